```python
import jax, jax.numpy as jnp
from jax import lax
import numpy as np

D_MODEL = 1024
BATCH = 2
SEQ = 16384
DEPTH = 4
DEC_BATCH = 1
DEC_SEQ = 16384
PAST_LEN = 128

GRID_W = 64
N_HEADS = 16
HEAD_DIM = D_MODEL // N_HEADS
WIN_H = 8
WIN_W = 16
Q_COL_BLOCK = 16
K_COL_BAND = 32
CONV_K = 31
FFN_CONV_K = 3
D_FF = 2816
N_MIXERS = 2
N_ATTN = (DEPTH + 1) // 2
N_CONV = DEPTH // 2
RMS_EPS = 1e-6
LN_EPS = 1e-5

kernel_name = 'hybrid_natten_conformer_encoder'


def rms_norm(x, g):
    x32 = x.astype(jnp.float32)
    y = x32 * lax.rsqrt(jnp.mean(x32 * x32, axis=-1, keepdims=True) + RMS_EPS)
    return y.astype(x.dtype) * g


def layer_norm(x, g, b):
    x32 = x.astype(jnp.float32)
    mu = jnp.mean(x32, axis=-1, keepdims=True)
    xc = x32 - mu
    var = jnp.mean(xc * xc, axis=-1, keepdims=True)
    return (xc * lax.rsqrt(var + LN_EPS)).astype(x.dtype) * g + b


def depthwise_conv(x, w, b):
    y = lax.conv_general_dilated(x, w[:, None, :], window_strides=(1,), padding='SAME',
                                 dimension_numbers=('NWC', 'WIO', 'NWC'),
                                 feature_group_count=x.shape[-1])
    return y + b


def neighbourhood_attention(x, w_qkv, b_qkv, rpb, w_o, b_o):
    bsz, seq, _ = x.shape
    rows = seq // GRID_W
    kh = min(WIN_H, rows)
    n_cb = GRID_W // Q_COL_BLOCK
    qkv = x @ w_qkv + b_qkv
    q, k, v = jnp.split(qkv, 3, axis=-1)
    q = q * (HEAD_DIM ** -0.5)
    shape5 = (bsz, rows, GRID_W, N_HEADS, HEAD_DIM)
    q, k, v = q.reshape(shape5), k.reshape(shape5), v.reshape(shape5)
    q_col = np.arange(GRID_W).reshape(n_cb, Q_COL_BLOCK)
    band_start = np.clip(q_col[:, 0] - WIN_W // 2, 0, GRID_W - K_COL_BAND)
    key_col = band_start[:, None] + np.arange(K_COL_BAND)[None, :]
    win_start = np.clip(q_col - WIN_W // 2, 0, GRID_W - WIN_W)
    kc = key_col[:, None, :]
    ws = win_start[..., None]
    col_ok = (kc >= ws) & (kc < ws + WIN_W)
    dc_idx = np.clip(kc - q_col[..., None] + WIN_W - 1, 0, 2 * WIN_W - 2)
    col_bias = rpb[:, :, dc_idx]
    mask = jnp.asarray(col_ok)[:, :, None, :]

    def one_row(r):
        rs = jnp.clip(r - kh // 2, 0, rows - kh)
        k_band = lax.dynamic_slice_in_dim(k, rs, kh, axis=1)[:, :, key_col]
        v_band = lax.dynamic_slice_in_dim(v, rs, kh, axis=1)[:, :, key_col]
        q_blk = q[:, r].reshape(bsz, n_cb, Q_COL_BLOCK, N_HEADS, HEAD_DIM)
        s = jnp.einsum('bjqhd,bkjlhd->bhjqkl', q_blk, k_band)
        dr_idx = rs + jnp.arange(kh) - r + (WIN_H - 1)
        bias = jnp.take(col_bias, dr_idx, axis=1).transpose(0, 2, 3, 1, 4)
        s = s.astype(jnp.float32) + bias.astype(jnp.float32)
        s = jnp.where(mask, s, -jnp.inf)
        p = jax.nn.softmax(s.reshape(s.shape[:4] + (kh * K_COL_BAND,)), axis=-1)
        p = p.reshape(s.shape).astype(v.dtype)
        o = jnp.einsum('bhjqkl,bkjlhd->bjqhd', p, v_band)
        return o.reshape(bsz, GRID_W, D_MODEL)

    out = lax.map(one_row, jnp.arange(rows))
    out = out.transpose(1, 0, 2, 3).reshape(bsz, seq, D_MODEL)
    return out @ w_o + b_o


def conformer_conv(x, w_pw1, b_pw1, w_dw, b_dw, ln_g, ln_b, w_pw2, b_pw2):
    h = x @ w_pw1 + b_pw1
    a, g = jnp.split(h, 2, axis=-1)
    h = a * jax.nn.sigmoid(g)
    h = depthwise_conv(h, w_dw, b_dw)
    h = jax.nn.silu(layer_norm(h, ln_g, ln_b))
    return h @ w_pw2 + b_pw2


def conv_glu_ffn(x, w_up, w_dw, b_dw, w_down):
    h = depthwise_conv(x @ w_up, w_dw, b_dw)
    g, u = jnp.split(h, 2, axis=-1)
    return (jax.nn.silu(g) * u) @ w_down


def trunk(x, attn_w_qkv, attn_b_qkv, attn_rpb, attn_w_o, attn_b_o,
          conv_w_pw1, conv_b_pw1, conv_w_dw, conv_b_dw, conv_ln_g, conv_ln_b, conv_w_pw2, conv_b_pw2,
          ffn_w_up, ffn_w_dw, ffn_b_dw, ffn_w_down, norm_mix, norm_ffn, norm_final):
    for i in range(DEPTH):
        h = rms_norm(x, norm_mix[i])
        j = i // N_MIXERS
        if i % N_MIXERS == 0:
            x = x + neighbourhood_attention(h, attn_w_qkv[j], attn_b_qkv[j], attn_rpb[j], attn_w_o[j], attn_b_o[j])
        else:
            x = x + conformer_conv(h, conv_w_pw1[j], conv_b_pw1[j], conv_w_dw[j], conv_b_dw[j],
                                   conv_ln_g[j], conv_ln_b[j], conv_w_pw2[j], conv_b_pw2[j])
        x = x + conv_glu_ffn(rms_norm(x, norm_ffn[i]), ffn_w_up[i], ffn_w_dw[i], ffn_b_dw[i], ffn_w_down[i])
    return rms_norm(x, norm_final)


def setup_inputs(seed: int = 0) -> dict:
    key = jax.random.key(seed)
    ks = jax.random.split(key, 22)
    D, F, H = D_MODEL, D_FF, N_HEADS
    res_scale = (2 * DEPTH) ** -0.5
    nrm = lambda k, shape, s: jax.random.normal(k, shape, jnp.float32) * s
    return {
        'x_prompt': nrm(ks[0], (BATCH, SEQ, D), 1.0),
        'x_sample': nrm(ks[1], (DEC_BATCH, DEC_SEQ, D), 1.0),
        'attn_w_qkv': nrm(ks[2], (N_ATTN, D, 3 * D), D ** -0.5),
        'attn_b_qkv': nrm(ks[3], (N_ATTN, 3 * D), 0.02),
        'attn_rpb': nrm(ks[4], (N_ATTN, H, 2 * WIN_H - 1, 2 * WIN_W - 1), 0.5),
        'attn_w_o': nrm(ks[5], (N_ATTN, D, D), D ** -0.5 * res_scale),
        'attn_b_o': nrm(ks[6], (N_ATTN, D), 0.02),
        'conv_w_pw1': nrm(ks[7], (N_CONV, D, 2 * D), D ** -0.5),
        'conv_b_pw1': nrm(ks[8], (N_CONV, 2 * D), 0.02),
        'conv_w_dw': nrm(ks[9], (N_CONV, CONV_K, D), CONV_K ** -0.5),
        'conv_b_dw': nrm(ks[10], (N_CONV, D), 0.02),
        'conv_ln_g': 1.0 + nrm(ks[11], (N_CONV, D), 0.05),
        'conv_ln_b': nrm(ks[12], (N_CONV, D), 0.02),
        'conv_w_pw2': nrm(ks[13], (N_CONV, D, D), D ** -0.5 * res_scale),
        'conv_b_pw2': nrm(ks[14], (N_CONV, D), 0.02),
        'ffn_w_up': nrm(ks[15], (DEPTH, D, 2 * F), D ** -0.5),
        'ffn_w_dw': nrm(ks[16], (DEPTH, FFN_CONV_K, 2 * F), FFN_CONV_K ** -0.5),
        'ffn_b_dw': nrm(ks[17], (DEPTH, 2 * F), 0.02),
        'ffn_w_down': nrm(ks[18], (DEPTH, F, D), F ** -0.5 * res_scale),
        'norm_mix': 1.0 + nrm(ks[19], (DEPTH, D), 0.05),
        'norm_ffn': 1.0 + nrm(ks[20], (DEPTH, D), 0.05),
        'norm_final': 1.0 + nrm(ks[21], (D,), 0.05),
    }


def reference(x_prompt, x_sample, attn_w_qkv, attn_b_qkv, attn_rpb, attn_w_o, attn_b_o,
              conv_w_pw1, conv_b_pw1, conv_w_dw, conv_b_dw, conv_ln_g, conv_ln_b, conv_w_pw2, conv_b_pw2,
              ffn_w_up, ffn_w_dw, ffn_b_dw, ffn_w_down, norm_mix, norm_ffn, norm_final):
    y_prompt = trunk(x_prompt, attn_w_qkv, attn_b_qkv, attn_rpb, attn_w_o, attn_b_o,
                     conv_w_pw1, conv_b_pw1, conv_w_dw, conv_b_dw, conv_ln_g, conv_ln_b, conv_w_pw2, conv_b_pw2,
                     ffn_w_up, ffn_w_dw, ffn_b_dw, ffn_w_down, norm_mix, norm_ffn, norm_final)
    y_sample = trunk(x_sample, attn_w_qkv, attn_b_qkv, attn_rpb, attn_w_o, attn_b_o,
                     conv_w_pw1, conv_b_pw1, conv_w_dw, conv_b_dw, conv_ln_g, conv_ln_b, conv_w_pw2, conv_b_pw2,
                     ffn_w_up, ffn_w_dw, ffn_b_dw, ffn_w_down, norm_mix, norm_ffn, norm_final)
    return (y_prompt, y_sample)
```

```python
import functools

import numpy as np
import jax
import jax.numpy as jnp
from jax import lax
from jax.experimental import pallas as pl
from jax.experimental.pallas import tpu as pltpu

D_MODEL = 1024
SEQ = 16384
GRID_W = 64
ROWS = SEQ // GRID_W
N_HEADS = 16
HEAD_DIM = D_MODEL // N_HEADS
WIN_H = 8
WIN_W = 16
CONV_K = 31
FFN_CONV_K = 3
D_FF = 2816
RMS_EPS = 1e-6
LN_EPS = 1e-5

LANES = 128
SUBLANES = 8
N_PAIRS = D_MODEL // LANES
VMEM_LIMIT = 56 * 1024 * 1024

QKV_TM = 512
ATTN_RB = 16
ATTN_HALO = 8
PROJ_TM = 512
FFN_TM = 512
FFN_HALO = SUBLANES
FFN_FC = 256
CONV_TM = 256
CONV_HALO = 16
CONV_CC = 256
CONV_R = 32

BF16 = jnp.bfloat16
F32 = jnp.float32


def _rms(x, g):
    ms = jnp.mean(x * x, axis=-1, keepdims=True)
    return (x * lax.rsqrt(ms + RMS_EPS)) * g


def _params(n_grid):
    return pltpu.CompilerParams(
        dimension_semantics=("arbitrary",) * n_grid, vmem_limit_bytes=VMEM_LIMIT)


def _whole(shape):
    return pl.BlockSpec(shape, lambda *_: (0,) * len(shape), pipeline_mode=pl.Buffered(1))


def _qkv_kernel(x_ref, g_ref, w_ref, b_ref, o_ref):
    hb = _rms(x_ref[...], g_ref[...]).astype(BF16)
    nc = 2 * LANES
    for n in range(3 * D_MODEL // nc):
        acc = jnp.dot(hb, w_ref[:, n * nc:(n + 1) * nc], preferred_element_type=F32)
        acc = acc + b_ref[:, n * nc:(n + 1) * nc]
        if n * nc < D_MODEL:
            acc = acc * (HEAD_DIM ** -0.5)
        o_ref[2 * n] = acc[:, :LANES].astype(BF16)
        o_ref[2 * n + 1] = acc[:, LANES:].astype(BF16)


def _qkv(x, g, w, b):
    t = x.shape[0]
    return pl.pallas_call(
        _qkv_kernel,
        grid=(t // QKV_TM,),
        in_specs=[
            pl.BlockSpec((QKV_TM, D_MODEL), lambda i: (i, 0)),
            _whole((1, D_MODEL)),
            _whole((D_MODEL, 3 * D_MODEL)),
            _whole((1, 3 * D_MODEL)),
        ],
        out_specs=pl.BlockSpec((3 * N_PAIRS, QKV_TM, LANES), lambda i: (0, i, 0)),
        out_shape=jax.ShapeDtypeStruct((3 * N_PAIRS, t, LANES), BF16),
        compiler_params=_params(1),
        name="qkv",
    )(x, g, w, b)


def _attn_kernel(q_ref, kp_ref, km_ref, kn_ref, vp_ref, vm_ref, vn_ref, tab_ref, o_ref,
                 kwin, vwin):
    halo = ATTN_HALO * GRID_W
    main = ATTN_RB * GRID_W
    kwin[0:halo] = kp_ref[0]
    kwin[halo:halo + main] = km_ref[0]
    kwin[halo + main:] = kn_ref[0]
    vwin[0:halo] = vp_ref[0]
    vwin[halo:halo + main] = vm_ref[0]
    vwin[halo + main:] = vn_ref[0]

    r0 = pl.program_id(2) * ATTN_RB
    low = lax.broadcasted_iota(jnp.int32, (GRID_W, LANES), 1) < HEAD_DIM
    nkeys = WIN_H * GRID_W

    def row(rr, carry):
        r = r0 + rr
        rs = jnp.clip(r - WIN_H // 2, 0, ROWS - WIN_H)
        start = pl.multiple_of((rs - r0 + ATTN_HALO) * GRID_W, GRID_W)
        cls = rs - r + WIN_H - 1
        q = q_ref[0, pl.ds(pl.multiple_of(rr * GRID_W, GRID_W), GRID_W), :]
        zero = jnp.zeros_like(q)
        lhs = jnp.concatenate([jnp.where(low, q, zero), jnp.where(low, zero, q)], axis=0)
        kw = kwin[pl.ds(start, nkeys), :]
        vw = vwin[pl.ds(start, nkeys), :]
        s = lax.dot_general(lhs, kw, (((1,), (1,)), ((), ())), preferred_element_type=F32)
        s = s + tab_ref[cls, 0]
        m = jnp.max(s, axis=1, keepdims=True)
        p = jnp.exp(s - m)
        l = jnp.sum(p, axis=1, keepdims=True)
        pv = jnp.dot(p.astype(BF16), vw, preferred_element_type=F32) / l
        o = jnp.where(low, pv[:GRID_W], pv[GRID_W:])
        o_ref[0, pl.ds(pl.multiple_of(rr * GRID_W, GRID_W), GRID_W), :] = o.astype(BF16)
        return carry

    lax.fori_loop(0, ATTN_RB, row, 0)


def _attn(qkv, tab):
    t = qkv.shape[1]
    nseq = t // SEQ
    nblk = ROWS // ATTN_RB
    main = ATTN_RB * GRID_W
    halo = ATTN_HALO * GRID_W
    hps = ROWS // ATTN_HALO
    step = ATTN_RB // ATTN_HALO

    def mainspec(base):
        return pl.BlockSpec((1, main, LANES), lambda p, b, i: (base + p, b * nblk + i, 0))

    def prevspec(base):
        return pl.BlockSpec(
            (1, halo, LANES),
            lambda p, b, i: (base + p, b * hps + jnp.maximum(i * step - 1, 0), 0))

    def nextspec(base):
        return pl.BlockSpec(
            (1, halo, LANES),
            lambda p, b, i: (base + p, b * hps + jnp.minimum((i + 1) * step, hps - 1), 0))

    return pl.pallas_call(
        _attn_kernel,
        grid=(N_PAIRS, nseq, nblk),
        in_specs=[
            mainspec(0),
            prevspec(N_PAIRS), mainspec(N_PAIRS), nextspec(N_PAIRS),
            prevspec(2 * N_PAIRS), mainspec(2 * N_PAIRS), nextspec(2 * N_PAIRS),
            pl.BlockSpec((WIN_H, 1, 2 * GRID_W, WIN_H * GRID_W), lambda p, b, i: (0, p, 0, 0)),
        ],
        out_specs=pl.BlockSpec((1, main, LANES), lambda p, b, i: (p, b * nblk + i, 0)),
        out_shape=jax.ShapeDtypeStruct((N_PAIRS, t, LANES), BF16),
        scratch_shapes=[
            pltpu.VMEM((main + 2 * halo, LANES), BF16),
            pltpu.VMEM((main + 2 * halo, LANES), BF16),
        ],
        compiler_params=_params(3),
        name="natten",
    )(qkv, qkv, qkv, qkv, qkv, qkv, qkv, tab)


def _bias_table(rpb):
    c = np.arange(WIN_H)[:, None]
    kr = np.arange(WIN_H)[None, :]
    dr = c + kr
    qc = np.arange(GRID_W)[:, None]
    kc = np.arange(GRID_W)[None, :]
    ws = np.clip(qc - WIN_W // 2, 0, GRID_W - WIN_W)
    ok = (kc >= ws) & (kc < ws + WIN_W)
    dc = np.clip(kc - qc + WIN_W - 1, 0, 2 * WIN_W - 2)
    bias = rpb[:, dr[:, :, None, None], dc[None, None, :, :]]
    bias = jnp.where(jnp.asarray(ok)[None, None, None], bias, -jnp.inf)
    bias = bias.transpose(1, 0, 3, 2, 4)
    return bias.reshape(WIN_H, N_PAIRS, 2 * GRID_W, WIN_H * GRID_W)


def _proj_kernel(o_ref, x_ref, w_ref, b_ref, y_ref):
    o = jnp.concatenate([o_ref[p] for p in range(N_PAIRS)], axis=1)
    y = jnp.dot(o, w_ref[...], preferred_element_type=F32)
    y_ref[...] = x_ref[...] + (y + b_ref[...])


def _proj(o, x, w, b):
    t = x.shape[0]
    return pl.pallas_call(
        _proj_kernel,
        grid=(t // PROJ_TM,),
        in_specs=[
            pl.BlockSpec((N_PAIRS, PROJ_TM, LANES), lambda i: (0, i, 0)),
            pl.BlockSpec((PROJ_TM, D_MODEL), lambda i: (i, 0)),
            _whole((D_MODEL, D_MODEL)),
            _whole((1, D_MODEL)),
        ],
        out_specs=pl.BlockSpec((PROJ_TM, D_MODEL), lambda i: (i, 0)),
        out_shape=jax.ShapeDtypeStruct((t, D_MODEL), F32),
        compiler_params=_params(1),
        name="attn_out_proj",
    )(o, x, w, b)


def _halo_specs(tm, halo):
    per = tm // halo

    def prev(i):
        return (jnp.maximum(i * per - 1, 0), 0)

    def nxt(i, nb):
        return (jnp.minimum((i + 1) * per, nb - 1), 0)

    return prev, nxt


def _seq_edges(tm):
    tiles = SEQ // tm
    j = pl.program_id(0) % tiles
    return j == 0, j == tiles - 1


def _ffn_kernel(xp_ref, xm_ref, xn_ref, g_ref, wup_ref, wdw_ref, bdw_ref, wdn_ref, gf_ref,
                y_ref, act, *, final):
    first, last = _seq_edges(FFN_TM)
    g = g_ref[...]
    hp = jnp.where(first, 0.0, _rms(xp_ref[...], g))
    hn = jnp.where(last, 0.0, _rms(xn_ref[...], g))
    hb = jnp.concatenate([hp, _rms(xm_ref[...], g), hn], axis=0).astype(BF16)

    def conv(h, col):
        w = wdw_ref[:, pl.ds(col, FFN_FC)]
        out = bdw_ref[:, pl.ds(col, FFN_FC)]
        for k in range(FFN_CONV_K):
            lo = FFN_HALO + k - FFN_CONV_K // 2
            out = out + h[lo:lo + FFN_TM] * w[k:k + 1]
        return out

    def chunk(c, carry):
        cg = pl.multiple_of(c * FFN_FC, FFN_FC)
        cu = pl.multiple_of(D_FF + c * FFN_FC, LANES)
        hg = jnp.dot(hb, wup_ref[:, pl.ds(cg, FFN_FC)], preferred_element_type=F32)
        hu = jnp.dot(hb, wup_ref[:, pl.ds(cu, FFN_FC)], preferred_element_type=F32)
        gate = conv(hg, cg)
        up = conv(hu, cu)
        act[:, pl.ds(cg, FFN_FC)] = (gate * jax.nn.sigmoid(gate) * up).astype(BF16)
        return carry

    lax.fori_loop(0, D_FF // FFN_FC, chunk, 0)
    y = xm_ref[...] + jnp.dot(act[...], wdn_ref[...], preferred_element_type=F32)
    if final:
        y = _rms(y, gf_ref[...])
    y_ref[...] = y


def _ffn(x, g, wup, wdw, bdw, wdn, gfinal, final):
    t = x.shape[0]
    prev, nxt = _halo_specs(FFN_TM, FFN_HALO)
    nb = t // FFN_HALO
    return pl.pallas_call(
        functools.partial(_ffn_kernel, final=final),
        grid=(t // FFN_TM,),
        in_specs=[
            pl.BlockSpec((FFN_HALO, D_MODEL), prev),
            pl.BlockSpec((FFN_TM, D_MODEL), lambda i: (i, 0)),
            pl.BlockSpec((FFN_HALO, D_MODEL), lambda i: nxt(i, nb)),
            _whole((1, D_MODEL)),
            _whole((D_MODEL, 2 * D_FF)),
            _whole((FFN_CONV_K, 2 * D_FF)),
            _whole((1, 2 * D_FF)),
            _whole((D_FF, D_MODEL)),
            _whole((1, D_MODEL)),
        ],
        out_specs=pl.BlockSpec((FFN_TM, D_MODEL), lambda i: (i, 0)),
        out_shape=jax.ShapeDtypeStruct((t, D_MODEL), F32),
        scratch_shapes=[pltpu.VMEM((FFN_TM, D_FF), BF16)],
        compiler_params=_params(1),
        name="conv_glu_ffn",
    )(x, x, x, g, wup, wdw, bdw, wdn, gfinal)


def _cconv_kernel(xp_ref, xm_ref, xn_ref, g_ref, w1_ref, b1_ref, wdw_ref, bdw_ref,
                  lng_ref, lnb_ref, w2_ref, b2_ref, y_ref, glu, dw):
    first, last = _seq_edges(CONV_TM)
    g = g_ref[...]
    hb = jnp.concatenate(
        [_rms(xp_ref[...], g), _rms(xm_ref[...], g), _rms(xn_ref[...], g)], axis=0
    ).astype(BF16)
    ext = CONV_TM + 2 * CONV_HALO
    row = lax.broadcasted_iota(jnp.int32, (ext, 1), 0)
    pad = (first & (row < CONV_HALO)) | (last & (row >= CONV_HALO + CONV_TM))

    for c in range(D_MODEL // CONV_CC):
        ca, cg = c * CONV_CC, D_MODEL + c * CONV_CC
        a = jnp.dot(hb, w1_ref[:, ca:ca + CONV_CC], preferred_element_type=F32)
        a = a + b1_ref[:, ca:ca + CONV_CC]
        gt = jnp.dot(hb, w1_ref[:, cg:cg + CONV_CC], preferred_element_type=F32)
        gt = gt + b1_ref[:, cg:cg + CONV_CC]
        glu[:, ca:ca + CONV_CC] = jnp.where(pad, 0.0, a * jax.nn.sigmoid(gt))

    def dw_step(n, carry):
        t0 = pl.multiple_of((n // (D_MODEL // LANES)) * CONV_R, CONV_R)
        l0 = pl.multiple_of((n % (D_MODEL // LANES)) * LANES, LANES)
        src = glu[pl.ds(t0, CONV_R + 2 * CONV_HALO), pl.ds(l0, LANES)]
        w = wdw_ref[:, pl.ds(l0, LANES)]
        acc = jnp.zeros((CONV_R, LANES), F32) + bdw_ref[:, pl.ds(l0, LANES)]
        for k in range(CONV_K):
            lo = CONV_HALO + k - CONV_K // 2
            acc = acc + src[lo:lo + CONV_R, :] * w[k:k + 1]
        dw[pl.ds(t0, CONV_R), pl.ds(l0, LANES)] = acc
        return carry

    lax.fori_loop(0, (CONV_TM // CONV_R) * (D_MODEL // LANES), dw_step, 0)

    h = dw[...]
    mu = jnp.mean(h, axis=-1, keepdims=True)
    hc = h - mu
    var = jnp.mean(hc * hc, axis=-1, keepdims=True)
    hn = (hc * lax.rsqrt(var + LN_EPS)) * lng_ref[...] + lnb_ref[...]
    hs = (hn * jax.nn.sigmoid(hn)).astype(BF16)
    y = jnp.dot(hs, w2_ref[...], preferred_element_type=F32) + b2_ref[...]
    y_ref[...] = xm_ref[...] + y


def _cconv(x, g, w1, b1, wdw, bdw, lng, lnb, w2, b2):
    t = x.shape[0]
    prev, nxt = _halo_specs(CONV_TM, CONV_HALO)
    nb = t // CONV_HALO
    return pl.pallas_call(
        _cconv_kernel,
        grid=(t // CONV_TM,),
        in_specs=[
            pl.BlockSpec((CONV_HALO, D_MODEL), prev),
            pl.BlockSpec((CONV_TM, D_MODEL), lambda i: (i, 0)),
            pl.BlockSpec((CONV_HALO, D_MODEL), lambda i: nxt(i, nb)),
            _whole((1, D_MODEL)),
            _whole((D_MODEL, 2 * D_MODEL)),
            _whole((1, 2 * D_MODEL)),
            _whole((CONV_K, D_MODEL)),
            _whole((1, D_MODEL)),
            _whole((1, D_MODEL)),
            _whole((1, D_MODEL)),
            _whole((D_MODEL, D_MODEL)),
            _whole((1, D_MODEL)),
        ],
        out_specs=pl.BlockSpec((CONV_TM, D_MODEL), lambda i: (i, 0)),
        out_shape=jax.ShapeDtypeStruct((t, D_MODEL), F32),
        scratch_shapes=[
            pltpu.VMEM((CONV_TM + 2 * CONV_HALO, D_MODEL), F32),
            pltpu.VMEM((CONV_TM, D_MODEL), F32),
        ],
        compiler_params=_params(1),
        name="conformer_conv",
    )(x, x, x, g, w1, b1, wdw, bdw, lng, lnb, w2, b2)


def _trunk(x, attn_w_qkv, attn_b_qkv, attn_tab, attn_w_o, attn_b_o,
           conv_w_pw1, conv_b_pw1, conv_w_dw, conv_b_dw, conv_ln_g, conv_ln_b,
           conv_w_pw2, conv_b_pw2, ffn_w_up, ffn_w_dw, ffn_b_dw, ffn_w_down,
           norm_mix, norm_ffn, norm_final):
    depth = norm_mix.shape[0]
    for i in range(depth):
        j = i // 2
        if i % 2 == 0:
            qkv = _qkv(x, norm_mix[i][None], attn_w_qkv[j], attn_b_qkv[j][None])
            o = _attn(qkv, attn_tab[j])
            x = _proj(o, x, attn_w_o[j], attn_b_o[j][None])
        else:
            x = _cconv(x, norm_mix[i][None], conv_w_pw1[j], conv_b_pw1[j][None],
                       conv_w_dw[j], conv_b_dw[j][None], conv_ln_g[j][None],
                       conv_ln_b[j][None], conv_w_pw2[j], conv_b_pw2[j][None])
        x = _ffn(x, norm_ffn[i][None], ffn_w_up[i], ffn_w_dw[i], ffn_b_dw[i][None],
                 ffn_w_down[i], norm_final[None], final=(i == depth - 1))
    return x


def kernel(x_prompt, x_sample, attn_w_qkv, attn_b_qkv, attn_rpb, attn_w_o, attn_b_o,
           conv_w_pw1, conv_b_pw1, conv_w_dw, conv_b_dw, conv_ln_g, conv_ln_b,
           conv_w_pw2, conv_b_pw2, ffn_w_up, ffn_w_dw, ffn_b_dw, ffn_w_down,
           norm_mix, norm_ffn, norm_final):
    assert x_prompt.shape[1:] == (SEQ, D_MODEL) and x_sample.shape[1:] == (SEQ, D_MODEL)
    tab = jnp.stack([_bias_table(attn_rpb[j]) for j in range(attn_rpb.shape[0])])
    weights = (
        attn_w_qkv.astype(BF16), attn_b_qkv, tab, attn_w_o.astype(BF16), attn_b_o,
        conv_w_pw1.astype(BF16), conv_b_pw1, conv_w_dw, conv_b_dw, conv_ln_g, conv_ln_b,
        conv_w_pw2.astype(BF16), conv_b_pw2, ffn_w_up.astype(BF16), ffn_w_dw, ffn_b_dw,
        ffn_w_down.astype(BF16), norm_mix, norm_ffn, norm_final)
    outs = []
    for x in (x_prompt, x_sample):
        y = _trunk(x.reshape(-1, D_MODEL), *weights)
        outs.append(y.reshape(x.shape))
    return tuple(outs)
```

```python
import functools

import numpy as np
import jax
import jax.numpy as jnp
from jax import lax
from jax.experimental import pallas as pl
from jax.experimental.pallas import tpu as pltpu

D_MODEL = 1024
SEQ = 16384
GRID_W = 64
ROWS = SEQ // GRID_W
N_HEADS = 16
HEAD_DIM = D_MODEL // N_HEADS
WIN_H = 8
WIN_W = 16
CONV_K = 31
FFN_CONV_K = 3
D_FF = 2816
RMS_EPS = 1e-6
LN_EPS = 1e-5

LANES = 128
SUBLANES = 8
N_PAIRS = D_MODEL // LANES
VMEM_LIMIT = 56 * 1024 * 1024

QKV_TM = 512
ATTN_RB = 32
ATTN_HALO = 8
ATTN_MM_UNROLL = 8
ATTN_SM_UNROLL = 2
PROJ_TM = 512
FFN_TM = 512
FFN_HALO = SUBLANES
FFN_FC = 2 * LANES
CONV_TM = 256
CONV_HALO = 16
CONV_CC = 256
CONV_R = 64
CONV_SH = CONV_TM + 2 * CONV_HALO - SUBLANES

BF16 = jnp.bfloat16
F32 = jnp.float32


def _rms(x, g):
    ms = jnp.mean(x * x, axis=-1, keepdims=True)
    return (x * lax.rsqrt(ms + RMS_EPS)) * g


def _params(n_grid):
    return pltpu.CompilerParams(
        dimension_semantics=("arbitrary",) * n_grid, vmem_limit_bytes=VMEM_LIMIT)


def _whole(shape):
    return pl.BlockSpec(shape, lambda *_: (0,) * len(shape), pipeline_mode=pl.Buffered(1))


def _qkv_kernel(x_ref, g_ref, w_ref, b_ref, o_ref):
    hb = _rms(x_ref[...], g_ref[...]).astype(BF16)
    nc = 2 * LANES
    for n in range(3 * D_MODEL // nc):
        acc = jnp.dot(hb, w_ref[:, n * nc:(n + 1) * nc], preferred_element_type=F32)
        acc = acc + b_ref[:, n * nc:(n + 1) * nc]
        if n * nc < D_MODEL:
            acc = acc * (HEAD_DIM ** -0.5)
        o_ref[2 * n] = acc[:, :LANES].astype(BF16)
        o_ref[2 * n + 1] = acc[:, LANES:].astype(BF16)


def _qkv(x, g, w, b):
    t = x.shape[0]
    return pl.pallas_call(
        _qkv_kernel,
        grid=(t // QKV_TM,),
        in_specs=[
            pl.BlockSpec((QKV_TM, D_MODEL), lambda i: (i, 0)),
            _whole((1, D_MODEL)),
            _whole((D_MODEL, 3 * D_MODEL)),
            _whole((1, 3 * D_MODEL)),
        ],
        out_specs=pl.BlockSpec((3 * N_PAIRS, QKV_TM, LANES), lambda i: (0, i, 0)),
        out_shape=jax.ShapeDtypeStruct((3 * N_PAIRS, t, LANES), BF16),
        compiler_params=_params(1),
        name="qkv",
    )(x, g, w, b)


def _attn_kernel(q_ref, kp_ref, km_ref, kn_ref, vp_ref, vm_ref, vn_ref, tab_ref, o_ref,
                 kwin, vwin, vt, s_buf, p_buf, l_buf):
    halo = ATTN_HALO * GRID_W
    main = ATTN_RB * GRID_W
    kwin[0:halo] = kp_ref[0]
    kwin[halo:halo + main] = km_ref[0]
    kwin[halo + main:] = kn_ref[0]
    vwin[0:halo] = vp_ref[0]
    vwin[halo:halo + main] = vm_ref[0]
    vwin[halo + main:] = vn_ref[0]

    for par in range(2):
        for c in range((main + 2 * halo) // LANES - par):
            blk = vwin[par * GRID_W + c * LANES:par * GRID_W + (c + 1) * LANES, :]
            vt[par, :, c * LANES:(c + 1) * LANES] = blk.astype(F32).T.astype(BF16)

    r0 = pl.program_id(2) * ATTN_RB
    low = lax.broadcasted_iota(jnp.int32, (GRID_W, LANES), 1) < HEAD_DIM
    nkeys = WIN_H * GRID_W

    def first_key_row(rr):
        return jnp.clip(r0 + rr - WIN_H // 2, 0, ROWS - WIN_H)

    def scores(rr, carry):
        rs = first_key_row(rr)
        start = pl.multiple_of((rs - r0 + ATTN_HALO) * GRID_W, GRID_W)
        cls = rs - (r0 + rr) + WIN_H - 1
        q = q_ref[0, pl.ds(pl.multiple_of(rr * GRID_W, GRID_W), GRID_W), :]
        zero = jnp.zeros_like(q)
        qs = jnp.concatenate([jnp.where(low, q, zero), jnp.where(low, zero, q)], axis=0)
        kw = kwin[pl.ds(start, nkeys), :]
        st = lax.dot_general(kw, qs, (((1,), (1,)), ((), ())), preferred_element_type=F32)
        s_buf[pl.ds(pl.multiple_of(rr * nkeys, nkeys), nkeys), :] = st + tab_ref[cls, 0]
        return carry

    lax.fori_loop(0, ATTN_RB, scores, 0, unroll=ATTN_MM_UNROLL)

    nchunk = nkeys // GRID_W

    def softmax(rr, carry):
        base = rr * nkeys

        def chunk(j):
            return pl.ds(pl.multiple_of(base + j * GRID_W, GRID_W), GRID_W)

        acc = s_buf[chunk(0), :]
        for j in range(1, nchunk):
            acc = jnp.maximum(acc, s_buf[chunk(j), :])
        m = jnp.max(acc, axis=0, keepdims=True)
        tot = jnp.zeros((GRID_W, LANES), F32)
        for j in range(nchunk):
            p = jnp.exp(s_buf[chunk(j), :] - m)
            tot = tot + p
            p_buf[chunk(j), :] = p.astype(BF16)
        l = jnp.sum(tot, axis=0, keepdims=True)
        l_buf[pl.ds(pl.multiple_of(rr * SUBLANES, SUBLANES), SUBLANES), :] = (
            jnp.broadcast_to(l, (SUBLANES, LANES)))
        return carry

    lax.fori_loop(0, ATTN_RB, softmax, 0, unroll=ATTN_SM_UNROLL)

    def values(rr, carry):
        wrow = first_key_row(rr) - r0 + ATTN_HALO
        lane0 = pl.multiple_of((wrow // 2) * LANES, LANES)
        vw = vt[wrow % 2, :, pl.ds(lane0, nkeys)]
        pw = p_buf[pl.ds(pl.multiple_of(rr * nkeys, nkeys), nkeys), :]
        ot = jnp.dot(vw, pw, preferred_element_type=F32)
        l = l_buf[pl.ds(pl.multiple_of(rr * SUBLANES, SUBLANES), SUBLANES), :]
        o = (ot / l[0:1]).T
        o = jnp.where(low, o[:GRID_W], o[GRID_W:])
        o_ref[0, pl.ds(pl.multiple_of(rr * GRID_W, GRID_W), GRID_W), :] = o.astype(BF16)
        return carry

    lax.fori_loop(0, ATTN_RB, values, 0, unroll=ATTN_MM_UNROLL)


def _attn(qkv, tab):
    t = qkv.shape[1]
    nseq = t // SEQ
    nblk = ROWS // ATTN_RB
    main = ATTN_RB * GRID_W
    halo = ATTN_HALO * GRID_W
    hps = ROWS // ATTN_HALO
    step = ATTN_RB // ATTN_HALO

    def mainspec(base):
        return pl.BlockSpec((1, main, LANES), lambda p, b, i: (base + p, b * nblk + i, 0))

    def prevspec(base):
        return pl.BlockSpec(
            (1, halo, LANES),
            lambda p, b, i: (base + p, b * hps + jnp.maximum(i * step - 1, 0), 0))

    def nextspec(base):
        return pl.BlockSpec(
            (1, halo, LANES),
            lambda p, b, i: (base + p, b * hps + jnp.minimum((i + 1) * step, hps - 1), 0))

    return pl.pallas_call(
        _attn_kernel,
        grid=(N_PAIRS, nseq, nblk),
        in_specs=[
            mainspec(0),
            prevspec(N_PAIRS), mainspec(N_PAIRS), nextspec(N_PAIRS),
            prevspec(2 * N_PAIRS), mainspec(2 * N_PAIRS), nextspec(2 * N_PAIRS),
            pl.BlockSpec((WIN_H, 1, WIN_H * GRID_W, 2 * GRID_W), lambda p, b, i: (0, p, 0, 0)),
        ],
        out_specs=pl.BlockSpec((1, main, LANES), lambda p, b, i: (p, b * nblk + i, 0)),
        out_shape=jax.ShapeDtypeStruct((N_PAIRS, t, LANES), BF16),
        scratch_shapes=[
            pltpu.VMEM((main + 2 * halo, LANES), BF16),
            pltpu.VMEM((main + 2 * halo, LANES), BF16),
            pltpu.VMEM((2, LANES, main + 2 * halo), BF16),
            pltpu.VMEM((ATTN_RB * WIN_H * GRID_W, LANES), F32),
            pltpu.VMEM((ATTN_RB * WIN_H * GRID_W, LANES), BF16),
            pltpu.VMEM((ATTN_RB * SUBLANES, LANES), F32),
        ],
        compiler_params=_params(3),
        name="natten",
    )(qkv, qkv, qkv, qkv, qkv, qkv, qkv, tab)


def _bias_table(rpb):
    qc = np.arange(GRID_W)[:, None]
    kc = np.arange(GRID_W)[None, :]
    ws = np.clip(qc - WIN_W // 2, 0, GRID_W - WIN_W)
    ok = (kc >= ws) & (kc < ws + WIN_W)
    padw = GRID_W - WIN_W
    padded = jnp.pad(rpb, ((0, 0), (0, 0), (padw, padw)))
    t = jnp.stack([padded[:, :, GRID_W - 1 - q:2 * GRID_W - 1 - q] for q in range(GRID_W)],
                  axis=2)
    t = jnp.where(jnp.asarray(ok), t, -jnp.inf)
    tab = jnp.stack([t[:, c:c + WIN_H] for c in range(WIN_H)], axis=0)
    tab = tab.reshape(WIN_H, N_PAIRS, 2, WIN_H, GRID_W, GRID_W)
    tab = tab.transpose(0, 1, 3, 5, 2, 4)
    return tab.reshape(WIN_H, N_PAIRS, WIN_H * GRID_W, 2 * GRID_W)


def _proj_kernel(o_ref, x_ref, w_ref, b_ref, y_ref):
    o = jnp.concatenate([o_ref[p] for p in range(N_PAIRS)], axis=1)
    y = jnp.dot(o, w_ref[...], preferred_element_type=F32)
    y_ref[...] = x_ref[...] + (y + b_ref[...])


def _proj(o, x, w, b):
    t = x.shape[0]
    return pl.pallas_call(
        _proj_kernel,
        grid=(t // PROJ_TM,),
        in_specs=[
            pl.BlockSpec((N_PAIRS, PROJ_TM, LANES), lambda i: (0, i, 0)),
            pl.BlockSpec((PROJ_TM, D_MODEL), lambda i: (i, 0)),
            _whole((D_MODEL, D_MODEL)),
            _whole((1, D_MODEL)),
        ],
        out_specs=pl.BlockSpec((PROJ_TM, D_MODEL), lambda i: (i, 0)),
        out_shape=jax.ShapeDtypeStruct((t, D_MODEL), F32),
        compiler_params=_params(1),
        name="attn_out_proj",
    )(o, x, w, b)


def _halo_specs(tm, halo):
    per = tm // halo

    def prev(i):
        return (jnp.maximum(i * per - 1, 0), 0)

    def nxt(i, nb):
        return (jnp.minimum((i + 1) * per, nb - 1), 0)

    return prev, nxt


def _seq_edges(tm):
    tiles = SEQ // tm
    j = pl.program_id(0) % tiles
    return j == 0, j == tiles - 1


def _ffn_kernel(xp_ref, xm_ref, xn_ref, g_ref, wup_ref, wdw_ref, bdw_ref, wdn_ref, gf_ref,
                y_ref, hb, h0, h1, act, *, final):
    first, last = _seq_edges(FFN_TM)
    g = g_ref[...]
    hp = jnp.where(first, 0.0, _rms(xp_ref[...], g))
    hn = jnp.where(last, 0.0, _rms(xn_ref[...], g))
    hb[...] = jnp.concatenate([hp, _rms(xm_ref[...], g), hn], axis=0).astype(BF16)
    nchunk = D_FF // LANES

    def up_matmul(c, dst):
        col = pl.multiple_of(c * FFN_FC, FFN_FC)
        dst[...] = jnp.dot(hb[...], wup_ref[:, pl.ds(col, FFN_FC)], preferred_element_type=F32)

    def activate(c, src):
        col = pl.multiple_of(c * FFN_FC, FFN_FC)
        w = wdw_ref[:, pl.ds(col, FFN_FC)]
        out = bdw_ref[:, pl.ds(col, FFN_FC)]
        h = src[...]
        for k in range(FFN_CONV_K):
            lo = FFN_HALO + k - FFN_CONV_K // 2
            out = out + h[lo:lo + FFN_TM] * w[k:k + 1]
        gate, up = out[:, :LANES], out[:, LANES:]
        act[:, pl.ds(pl.multiple_of(c * LANES, LANES), LANES)] = (
            gate * jax.nn.sigmoid(gate) * up).astype(BF16)

    def pair(i, carry):
        up_matmul(2 * i + 1, h1)
        activate(2 * i, h0)
        up_matmul(jnp.minimum(2 * i + 2, nchunk - 1), h0)
        activate(2 * i + 1, h1)
        return carry

    up_matmul(0, h0)
    lax.fori_loop(0, nchunk // 2, pair, 0)
    y = xm_ref[...] + jnp.dot(act[...], wdn_ref[...], preferred_element_type=F32)
    if final:
        y = _rms(y, gf_ref[...])
    y_ref[...] = y


def _ffn(x, g, wup, wdw, bdw, wdn, gfinal, final):
    t = x.shape[0]
    prev, nxt = _halo_specs(FFN_TM, FFN_HALO)
    nb = t // FFN_HALO
    return pl.pallas_call(
        functools.partial(_ffn_kernel, final=final),
        grid=(t // FFN_TM,),
        in_specs=[
            pl.BlockSpec((FFN_HALO, D_MODEL), prev),
            pl.BlockSpec((FFN_TM, D_MODEL), lambda i: (i, 0)),
            pl.BlockSpec((FFN_HALO, D_MODEL), lambda i: nxt(i, nb)),
            _whole((1, D_MODEL)),
            _whole((D_MODEL, 2 * D_FF)),
            _whole((FFN_CONV_K, 2 * D_FF)),
            _whole((1, 2 * D_FF)),
            _whole((D_FF, D_MODEL)),
            _whole((1, D_MODEL)),
        ],
        out_specs=pl.BlockSpec((FFN_TM, D_MODEL), lambda i: (i, 0)),
        out_shape=jax.ShapeDtypeStruct((t, D_MODEL), F32),
        scratch_shapes=[
            pltpu.VMEM((FFN_TM + 2 * FFN_HALO, D_MODEL), BF16),
            pltpu.VMEM((FFN_TM + 2 * FFN_HALO, FFN_FC), F32),
            pltpu.VMEM((FFN_TM + 2 * FFN_HALO, FFN_FC), F32),
            pltpu.VMEM((FFN_TM, D_FF), BF16),
        ],
        compiler_params=_params(1),
        name="conv_glu_ffn",
    )(x, x, x, g, wup, wdw, bdw, wdn, gfinal)


def _chunk_gate_up(w):
    lead = w.shape[:-1]
    w = w.reshape(lead + (2, D_FF // LANES, LANES))
    return jnp.swapaxes(w, -3, -2).reshape(lead + (2 * D_FF,))


def _cconv_kernel(xp_ref, xm_ref, xn_ref, g_ref, w1_ref, b1_ref, wdw_ref, bdw_ref,
                  lng_ref, lnb_ref, w2_ref, b2_ref, y_ref, glu, dw):
    first, last = _seq_edges(CONV_TM)
    g = g_ref[...]
    hb = jnp.concatenate(
        [_rms(xp_ref[...], g), _rms(xm_ref[...], g), _rms(xn_ref[...], g)], axis=0
    ).astype(BF16)
    ext = CONV_TM + 2 * CONV_HALO
    row = lax.broadcasted_iota(jnp.int32, (ext, 1), 0)
    pad = (first & (row < CONV_HALO)) | (last & (row >= CONV_HALO + CONV_TM))

    for c in range(D_MODEL // CONV_CC):
        ca, cg = c * CONV_CC, D_MODEL + c * CONV_CC
        a = jnp.dot(hb, w1_ref[:, ca:ca + CONV_CC], preferred_element_type=F32)
        a = a + b1_ref[:, ca:ca + CONV_CC]
        gt = jnp.dot(hb, w1_ref[:, cg:cg + CONV_CC], preferred_element_type=F32)
        gt = gt + b1_ref[:, cg:cg + CONV_CC]
        gl = jnp.where(pad, 0.0, a * jax.nn.sigmoid(gt))
        for b in range(SUBLANES):
            for j in range(CONV_CC // LANES):
                glu[b, ca // LANES + j] = gl[b:b + CONV_SH, j * LANES:(j + 1) * LANES]

    def dw_step(n, carry):
        t0 = (n // (D_MODEL // LANES)) * CONV_R
        lt = n % (D_MODEL // LANES)
        l0 = pl.multiple_of(lt * LANES, LANES)
        w = wdw_ref[:, pl.ds(l0, LANES)]
        acc = jnp.zeros((CONV_R, LANES), F32) + bdw_ref[:, pl.ds(l0, LANES)]
        for k in range(CONV_K):
            a, b = divmod(CONV_HALO + k - CONV_K // 2, SUBLANES)
            rows = pl.ds(pl.multiple_of(t0 + a * SUBLANES, SUBLANES), CONV_R)
            acc = acc + glu[b, lt, rows, :] * w[k:k + 1]
        dw[pl.ds(pl.multiple_of(t0, CONV_R), CONV_R), pl.ds(l0, LANES)] = acc
        return carry

    lax.fori_loop(0, (CONV_TM // CONV_R) * (D_MODEL // LANES), dw_step, 0)

    h = dw[...]
    mu = jnp.mean(h, axis=-1, keepdims=True)
    hc = h - mu
    var = jnp.mean(hc * hc, axis=-1, keepdims=True)
    hn = (hc * lax.rsqrt(var + LN_EPS)) * lng_ref[...] + lnb_ref[...]
    hs = (hn * jax.nn.sigmoid(hn)).astype(BF16)
    y = jnp.dot(hs, w2_ref[...], preferred_element_type=F32) + b2_ref[...]
    y_ref[...] = xm_ref[...] + y


def _cconv(x, g, w1, b1, wdw, bdw, lng, lnb, w2, b2):
    t = x.shape[0]
    prev, nxt = _halo_specs(CONV_TM, CONV_HALO)
    nb = t // CONV_HALO
    return pl.pallas_call(
        _cconv_kernel,
        grid=(t // CONV_TM,),
        in_specs=[
            pl.BlockSpec((CONV_HALO, D_MODEL), prev),
            pl.BlockSpec((CONV_TM, D_MODEL), lambda i: (i, 0)),
            pl.BlockSpec((CONV_HALO, D_MODEL), lambda i: nxt(i, nb)),
            _whole((1, D_MODEL)),
            _whole((D_MODEL, 2 * D_MODEL)),
            _whole((1, 2 * D_MODEL)),
            _whole((CONV_K, D_MODEL)),
            _whole((1, D_MODEL)),
            _whole((1, D_MODEL)),
            _whole((1, D_MODEL)),
            _whole((D_MODEL, D_MODEL)),
            _whole((1, D_MODEL)),
        ],
        out_specs=pl.BlockSpec((CONV_TM, D_MODEL), lambda i: (i, 0)),
        out_shape=jax.ShapeDtypeStruct((t, D_MODEL), F32),
        scratch_shapes=[
            pltpu.VMEM((SUBLANES, D_MODEL // LANES, CONV_SH, LANES), F32),
            pltpu.VMEM((CONV_TM, D_MODEL), F32),
        ],
        compiler_params=_params(1),
        name="conformer_conv",
    )(x, x, x, g, w1, b1, wdw, bdw, lng, lnb, w2, b2)


def _trunk(x, attn_w_qkv, attn_b_qkv, attn_tab, attn_w_o, attn_b_o,
           conv_w_pw1, conv_b_pw1, conv_w_dw, conv_b_dw, conv_ln_g, conv_ln_b,
           conv_w_pw2, conv_b_pw2, ffn_w_up, ffn_w_dw, ffn_b_dw, ffn_w_down,
           norm_mix, norm_ffn, norm_final):
    depth = norm_mix.shape[0]
    for i in range(depth):
        j = i // 2
        if i % 2 == 0:
            qkv = _qkv(x, norm_mix[i][None], attn_w_qkv[j], attn_b_qkv[j][None])
            o = _attn(qkv, attn_tab[j])
            x = _proj(o, x, attn_w_o[j], attn_b_o[j][None])
        else:
            x = _cconv(x, norm_mix[i][None], conv_w_pw1[j], conv_b_pw1[j][None],
                       conv_w_dw[j], conv_b_dw[j][None], conv_ln_g[j][None],
                       conv_ln_b[j][None], conv_w_pw2[j], conv_b_pw2[j][None])
        x = _ffn(x, norm_ffn[i][None], ffn_w_up[i], ffn_w_dw[i], ffn_b_dw[i][None],
                 ffn_w_down[i], norm_final[None], final=(i == depth - 1))
    return x


def kernel(x_prompt, x_sample, attn_w_qkv, attn_b_qkv, attn_rpb, attn_w_o, attn_b_o,
           conv_w_pw1, conv_b_pw1, conv_w_dw, conv_b_dw, conv_ln_g, conv_ln_b,
           conv_w_pw2, conv_b_pw2, ffn_w_up, ffn_w_dw, ffn_b_dw, ffn_w_down,
           norm_mix, norm_ffn, norm_final):
    assert x_prompt.shape[1:] == (SEQ, D_MODEL) and x_sample.shape[1:] == (SEQ, D_MODEL)
    tab = jnp.stack([_bias_table(attn_rpb[j]) for j in range(attn_rpb.shape[0])])
    weights = (
        attn_w_qkv.astype(BF16), attn_b_qkv, tab, attn_w_o.astype(BF16), attn_b_o,
        conv_w_pw1.astype(BF16), conv_b_pw1, conv_w_dw, conv_b_dw, conv_ln_g, conv_ln_b,
        conv_w_pw2.astype(BF16), conv_b_pw2, _chunk_gate_up(ffn_w_up).astype(BF16),
        _chunk_gate_up(ffn_w_dw), _chunk_gate_up(ffn_b_dw),
        ffn_w_down.astype(BF16), norm_mix, norm_ffn, norm_final)
    outs = []
    for x in (x_prompt, x_sample):
        y = _trunk(x.reshape(-1, D_MODEL), *weights)
        outs.append(y.reshape(x.shape))
    return tuple(outs)
```

```python
import functools

import numpy as np
import jax
import jax.numpy as jnp
from jax import lax
from jax.experimental import pallas as pl
from jax.experimental.pallas import tpu as pltpu

D_MODEL = 1024
SEQ = 16384
GRID_W = 64
ROWS = SEQ // GRID_W
N_HEADS = 16
HEAD_DIM = D_MODEL // N_HEADS
WIN_H = 8
WIN_W = 16
CONV_K = 31
FFN_CONV_K = 3
D_FF = 2816
RMS_EPS = 1e-6
LN_EPS = 1e-5

LANES = 128
SUBLANES = 8
N_PAIRS = D_MODEL // LANES
VMEM_LIMIT = 56 * 1024 * 1024

QKV_TM = 512
ATTN_RB = 32
ATTN_HALO = 8
ATTN_MM_UNROLL = 8
ATTN_SM_UNROLL = 2
PROJ_TM = 512
FFN_TM = 512
FFN_HALO = SUBLANES
FFN_FC = 2 * LANES
CONV_TM = 512
CONV_HALO = 16
CONV_CC = 256
CONV_R = 64
CONV_SH = CONV_TM + 2 * CONV_HALO - SUBLANES

BF16 = jnp.bfloat16
F32 = jnp.float32


def _rms(x, g):
    ms = jnp.mean(x * x, axis=-1, keepdims=True)
    return (x * lax.rsqrt(ms + RMS_EPS)) * g


def _params(n_grid):
    return pltpu.CompilerParams(
        dimension_semantics=("arbitrary",) * n_grid, vmem_limit_bytes=VMEM_LIMIT)


def _whole(shape):
    return pl.BlockSpec(shape, lambda *_: (0,) * len(shape), pipeline_mode=pl.Buffered(1))


def _qkv_kernel(x_ref, g_ref, w_ref, b_ref, o_ref):
    hb = _rms(x_ref[...], g_ref[...]).astype(BF16)
    nc = 2 * LANES
    for n in range(3 * D_MODEL // nc):
        acc = jnp.dot(hb, w_ref[:, n * nc:(n + 1) * nc], preferred_element_type=F32)
        acc = acc + b_ref[:, n * nc:(n + 1) * nc]
        if n * nc < D_MODEL:
            acc = acc * (HEAD_DIM ** -0.5)
        o_ref[2 * n] = acc[:, :LANES].astype(BF16)
        o_ref[2 * n + 1] = acc[:, LANES:].astype(BF16)


def _qkv(x, g, w, b):
    t = x.shape[0]
    return pl.pallas_call(
        _qkv_kernel,
        grid=(t // QKV_TM,),
        in_specs=[
            pl.BlockSpec((QKV_TM, D_MODEL), lambda i: (i, 0)),
            _whole((1, D_MODEL)),
            _whole((D_MODEL, 3 * D_MODEL)),
            _whole((1, 3 * D_MODEL)),
        ],
        out_specs=pl.BlockSpec((3 * N_PAIRS, QKV_TM, LANES), lambda i: (0, i, 0)),
        out_shape=jax.ShapeDtypeStruct((3 * N_PAIRS, t, LANES), BF16),
        compiler_params=_params(1),
        name="qkv",
    )(x, g, w, b)


def _attn_kernel(q_ref, kp_ref, km_ref, kn_ref, vp_ref, vm_ref, vn_ref, tab_ref, o_ref,
                 kwin, vwin, vt, s_buf, p_buf, l_buf):
    halo = ATTN_HALO * GRID_W
    main = ATTN_RB * GRID_W
    kwin[0:halo] = kp_ref[0]
    kwin[halo:halo + main] = km_ref[0]
    kwin[halo + main:] = kn_ref[0]
    vwin[0:halo] = vp_ref[0]
    vwin[halo:halo + main] = vm_ref[0]
    vwin[halo + main:] = vn_ref[0]

    for par in range(2):
        for c in range((main + 2 * halo) // LANES - par):
            blk = vwin[par * GRID_W + c * LANES:par * GRID_W + (c + 1) * LANES, :]
            vt[par, :, c * LANES:(c + 1) * LANES] = blk.astype(F32).T.astype(BF16)

    r0 = pl.program_id(2) * ATTN_RB
    low = lax.broadcasted_iota(jnp.int32, (GRID_W, LANES), 1) < HEAD_DIM
    nkeys = WIN_H * GRID_W

    def first_key_row(rr):
        return jnp.clip(r0 + rr - WIN_H // 2, 0, ROWS - WIN_H)

    def scores(rr, carry):
        rs = first_key_row(rr)
        start = pl.multiple_of((rs - r0 + ATTN_HALO) * GRID_W, GRID_W)
        cls = rs - (r0 + rr) + WIN_H - 1
        q = q_ref[0, pl.ds(pl.multiple_of(rr * GRID_W, GRID_W), GRID_W), :]
        zero = jnp.zeros_like(q)
        qs = jnp.concatenate([jnp.where(low, q, zero), jnp.where(low, zero, q)], axis=0)
        kw = kwin[pl.ds(start, nkeys), :]
        st = lax.dot_general(kw, qs, (((1,), (1,)), ((), ())), preferred_element_type=F32)
        bias = tab_ref[0, pl.ds(pl.multiple_of(cls * GRID_W, GRID_W), nkeys), :]
        s_buf[pl.ds(pl.multiple_of(rr * nkeys, nkeys), nkeys), :] = st + bias
        return carry

    lax.fori_loop(0, ATTN_RB, scores, 0, unroll=ATTN_MM_UNROLL)

    nchunk = nkeys // GRID_W

    def softmax(rr, carry):
        base = rr * nkeys

        def chunk(j):
            return pl.ds(pl.multiple_of(base + j * GRID_W, GRID_W), GRID_W)

        acc = s_buf[chunk(0), :]
        for j in range(1, nchunk):
            acc = jnp.maximum(acc, s_buf[chunk(j), :])
        m = jnp.max(acc, axis=0, keepdims=True)
        tot = jnp.zeros((GRID_W, LANES), F32)
        for j in range(nchunk):
            p = jnp.exp(s_buf[chunk(j), :] - m)
            tot = tot + p
            p_buf[chunk(j), :] = p.astype(BF16)
        l = jnp.sum(tot, axis=0, keepdims=True)
        l_buf[pl.ds(pl.multiple_of(rr * SUBLANES, SUBLANES), SUBLANES), :] = (
            jnp.broadcast_to(l, (SUBLANES, LANES)))
        return carry

    lax.fori_loop(0, ATTN_RB, softmax, 0, unroll=ATTN_SM_UNROLL)

    def values(rr, carry):
        wrow = first_key_row(rr) - r0 + ATTN_HALO
        lane0 = pl.multiple_of((wrow // 2) * LANES, LANES)
        vw = vt[wrow % 2, :, pl.ds(lane0, nkeys)]
        pw = p_buf[pl.ds(pl.multiple_of(rr * nkeys, nkeys), nkeys), :]
        ot = jnp.dot(vw, pw, preferred_element_type=F32)
        l = l_buf[pl.ds(pl.multiple_of(rr * SUBLANES, SUBLANES), SUBLANES), :]
        o = (ot / l[0:1]).T
        o = jnp.where(low, o[:GRID_W], o[GRID_W:])
        o_ref[0, pl.ds(pl.multiple_of(rr * GRID_W, GRID_W), GRID_W), :] = o.astype(BF16)
        return carry

    lax.fori_loop(0, ATTN_RB, values, 0, unroll=ATTN_MM_UNROLL)


def _attn(qkv, tab):
    t = qkv.shape[1]
    nseq = t // SEQ
    nblk = ROWS // ATTN_RB
    main = ATTN_RB * GRID_W
    halo = ATTN_HALO * GRID_W
    hps = ROWS // ATTN_HALO
    step = ATTN_RB // ATTN_HALO

    def mainspec(base):
        return pl.BlockSpec((1, main, LANES), lambda p, b, i: (base + p, b * nblk + i, 0))

    def prevspec(base):
        return pl.BlockSpec(
            (1, halo, LANES),
            lambda p, b, i: (base + p, b * hps + jnp.maximum(i * step - 1, 0), 0))

    def nextspec(base):
        return pl.BlockSpec(
            (1, halo, LANES),
            lambda p, b, i: (base + p, b * hps + jnp.minimum((i + 1) * step, hps - 1), 0))

    return pl.pallas_call(
        _attn_kernel,
        grid=(N_PAIRS, nseq, nblk),
        in_specs=[
            mainspec(0),
            prevspec(N_PAIRS), mainspec(N_PAIRS), nextspec(N_PAIRS),
            prevspec(2 * N_PAIRS), mainspec(2 * N_PAIRS), nextspec(2 * N_PAIRS),
            pl.BlockSpec((1, (2 * WIN_H - 1) * GRID_W, 2 * GRID_W), lambda p, b, i: (p, 0, 0)),
        ],
        out_specs=pl.BlockSpec((1, main, LANES), lambda p, b, i: (p, b * nblk + i, 0)),
        out_shape=jax.ShapeDtypeStruct((N_PAIRS, t, LANES), BF16),
        scratch_shapes=[
            pltpu.VMEM((main + 2 * halo, LANES), BF16),
            pltpu.VMEM((main + 2 * halo, LANES), BF16),
            pltpu.VMEM((2, LANES, main + 2 * halo), BF16),
            pltpu.VMEM((ATTN_RB * WIN_H * GRID_W, LANES), F32),
            pltpu.VMEM((ATTN_RB * WIN_H * GRID_W, LANES), BF16),
            pltpu.VMEM((ATTN_RB * SUBLANES, LANES), F32),
        ],
        compiler_params=_params(3),
        name="natten",
    )(qkv, qkv, qkv, qkv, qkv, qkv, qkv, tab)


def _bias_table(rpb):
    qc = np.arange(GRID_W)[:, None]
    kc = np.arange(GRID_W)[None, :]
    ws = np.clip(qc - WIN_W // 2, 0, GRID_W - WIN_W)
    ok = (kc >= ws) & (kc < ws + WIN_W)
    padw = GRID_W - WIN_W
    padded = jnp.pad(rpb, ((0, 0), (0, 0), (padw, padw)))
    t = jnp.stack([padded[:, :, GRID_W - 1 - q:2 * GRID_W - 1 - q] for q in range(GRID_W)],
                  axis=2)
    t = jnp.where(jnp.asarray(ok), t, -jnp.inf)
    ndr = 2 * WIN_H - 1
    tab = t.reshape(N_PAIRS, 2, ndr, GRID_W, GRID_W)
    tab = tab.transpose(0, 2, 4, 1, 3)
    return tab.reshape(N_PAIRS, ndr * GRID_W, 2 * GRID_W)


def _proj_kernel(o_ref, x_ref, w_ref, b_ref, y_ref):
    o = jnp.concatenate([o_ref[p] for p in range(N_PAIRS)], axis=1)
    y = jnp.dot(o, w_ref[...], preferred_element_type=F32)
    y_ref[...] = x_ref[...] + (y + b_ref[...])


def _proj(o, x, w, b):
    t = x.shape[0]
    return pl.pallas_call(
        _proj_kernel,
        grid=(t // PROJ_TM,),
        in_specs=[
            pl.BlockSpec((N_PAIRS, PROJ_TM, LANES), lambda i: (0, i, 0)),
            pl.BlockSpec((PROJ_TM, D_MODEL), lambda i: (i, 0)),
            _whole((D_MODEL, D_MODEL)),
            _whole((1, D_MODEL)),
        ],
        out_specs=pl.BlockSpec((PROJ_TM, D_MODEL), lambda i: (i, 0)),
        out_shape=jax.ShapeDtypeStruct((t, D_MODEL), F32),
        compiler_params=_params(1),
        name="attn_out_proj",
    )(o, x, w, b)


def _halo_specs(tm, halo):
    per = tm // halo

    def prev(i):
        return (jnp.maximum(i * per - 1, 0), 0)

    def nxt(i, nb):
        return (jnp.minimum((i + 1) * per, nb - 1), 0)

    return prev, nxt


def _seq_edges(tm):
    tiles = SEQ // tm
    j = pl.program_id(0) % tiles
    return j == 0, j == tiles - 1


def _ffn_kernel(xp_ref, xm_ref, xn_ref, g_ref, wup_ref, wdw_ref, bdw_ref, wdn_ref, gf_ref,
                y_ref, hb, act, *, final):
    first, last = _seq_edges(FFN_TM)
    g = g_ref[...]
    hp = jnp.where(first, 0.0, _rms(xp_ref[...], g))
    hn = jnp.where(last, 0.0, _rms(xn_ref[...], g))
    hb[...] = jnp.concatenate([hp, _rms(xm_ref[...], g), hn], axis=0).astype(BF16)

    def conv(cols):
        h = jnp.dot(hb[...], wup_ref[:, cols], preferred_element_type=F32)
        w = wdw_ref[:, cols]
        out = bdw_ref[:, cols]
        for k in range(FFN_CONV_K):
            lo = FFN_HALO + k - FFN_CONV_K // 2
            out = out + h[lo:lo + FFN_TM] * w[k:k + 1]
        return out

    def chunk(c, carry):
        gcols = pl.ds(pl.multiple_of(c * FFN_FC, FFN_FC), FFN_FC)
        ucols = pl.ds(pl.multiple_of(D_FF + c * FFN_FC, LANES), FFN_FC)
        gate = conv(gcols)
        up = conv(ucols)
        act[:, gcols] = (gate * jax.nn.sigmoid(gate) * up).astype(BF16)
        return carry

    lax.fori_loop(0, D_FF // FFN_FC, chunk, 0)
    y = xm_ref[...] + jnp.dot(act[...], wdn_ref[...], preferred_element_type=F32)
    if final:
        y = _rms(y, gf_ref[...])
    y_ref[...] = y


def _ffn(x, g, wup, wdw, bdw, wdn, gfinal, final):
    t = x.shape[0]
    prev, nxt = _halo_specs(FFN_TM, FFN_HALO)
    nb = t // FFN_HALO
    return pl.pallas_call(
        functools.partial(_ffn_kernel, final=final),
        grid=(t // FFN_TM,),
        in_specs=[
            pl.BlockSpec((FFN_HALO, D_MODEL), prev),
            pl.BlockSpec((FFN_TM, D_MODEL), lambda i: (i, 0)),
            pl.BlockSpec((FFN_HALO, D_MODEL), lambda i: nxt(i, nb)),
            _whole((1, D_MODEL)),
            _whole((D_MODEL, 2 * D_FF)),
            _whole((FFN_CONV_K, 2 * D_FF)),
            _whole((1, 2 * D_FF)),
            _whole((D_FF, D_MODEL)),
            _whole((1, D_MODEL)),
        ],
        out_specs=pl.BlockSpec((FFN_TM, D_MODEL), lambda i: (i, 0)),
        out_shape=jax.ShapeDtypeStruct((t, D_MODEL), F32),
        scratch_shapes=[
            pltpu.VMEM((FFN_TM + 2 * FFN_HALO, D_MODEL), BF16),
            pltpu.VMEM((FFN_TM, D_FF), BF16),
        ],
        compiler_params=_params(1),
        name="conv_glu_ffn",
    )(x, x, x, g, wup, wdw, bdw, wdn, gfinal)


def _cconv_kernel(xp_ref, xm_ref, xn_ref, g_ref, w1_ref, b1_ref, wdw_ref, bdw_ref,
                  lng_ref, lnb_ref, w2_ref, b2_ref, y_ref, glu, dw):
    first, last = _seq_edges(CONV_TM)
    g = g_ref[...]
    hb = jnp.concatenate(
        [_rms(xp_ref[...], g), _rms(xm_ref[...], g), _rms(xn_ref[...], g)], axis=0
    ).astype(BF16)
    ext = CONV_TM + 2 * CONV_HALO
    row = lax.broadcasted_iota(jnp.int32, (ext, 1), 0)
    pad = (first & (row < CONV_HALO)) | (last & (row >= CONV_HALO + CONV_TM))

    for c in range(D_MODEL // CONV_CC):
        ca, cg = c * CONV_CC, D_MODEL + c * CONV_CC
        a = jnp.dot(hb, w1_ref[:, ca:ca + CONV_CC], preferred_element_type=F32)
        a = a + b1_ref[:, ca:ca + CONV_CC]
        gt = jnp.dot(hb, w1_ref[:, cg:cg + CONV_CC], preferred_element_type=F32)
        gt = gt + b1_ref[:, cg:cg + CONV_CC]
        gl = jnp.where(pad, 0.0, a * jax.nn.sigmoid(gt))
        for b in range(SUBLANES):
            for j in range(CONV_CC // LANES):
                glu[b, ca // LANES + j] = gl[b:b + CONV_SH, j * LANES:(j + 1) * LANES]

    def dw_step(n, carry):
        t0 = (n // (D_MODEL // LANES)) * CONV_R
        lt = n % (D_MODEL // LANES)
        l0 = pl.multiple_of(lt * LANES, LANES)
        w = wdw_ref[:, pl.ds(l0, LANES)]
        acc = jnp.zeros((CONV_R, LANES), F32) + bdw_ref[:, pl.ds(l0, LANES)]
        for k in range(CONV_K):
            a, b = divmod(CONV_HALO + k - CONV_K // 2, SUBLANES)
            rows = pl.ds(pl.multiple_of(t0 + a * SUBLANES, SUBLANES), CONV_R)
            acc = acc + glu[b, lt, rows, :] * w[k:k + 1]
        dw[pl.ds(pl.multiple_of(t0, CONV_R), CONV_R), pl.ds(l0, LANES)] = acc
        return carry

    lax.fori_loop(0, (CONV_TM // CONV_R) * (D_MODEL // LANES), dw_step, 0)

    h = dw[...]
    mu = jnp.mean(h, axis=-1, keepdims=True)
    hc = h - mu
    var = jnp.mean(hc * hc, axis=-1, keepdims=True)
    hn = (hc * lax.rsqrt(var + LN_EPS)) * lng_ref[...] + lnb_ref[...]
    hs = (hn * jax.nn.sigmoid(hn)).astype(BF16)
    y = jnp.dot(hs, w2_ref[...], preferred_element_type=F32) + b2_ref[...]
    y_ref[...] = xm_ref[...] + y


def _cconv(x, g, w1, b1, wdw, bdw, lng, lnb, w2, b2):
    t = x.shape[0]
    prev, nxt = _halo_specs(CONV_TM, CONV_HALO)
    nb = t // CONV_HALO
    return pl.pallas_call(
        _cconv_kernel,
        grid=(t // CONV_TM,),
        in_specs=[
            pl.BlockSpec((CONV_HALO, D_MODEL), prev),
            pl.BlockSpec((CONV_TM, D_MODEL), lambda i: (i, 0)),
            pl.BlockSpec((CONV_HALO, D_MODEL), lambda i: nxt(i, nb)),
            _whole((1, D_MODEL)),
            _whole((D_MODEL, 2 * D_MODEL)),
            _whole((1, 2 * D_MODEL)),
            _whole((CONV_K, D_MODEL)),
            _whole((1, D_MODEL)),
            _whole((1, D_MODEL)),
            _whole((1, D_MODEL)),
            _whole((D_MODEL, D_MODEL)),
            _whole((1, D_MODEL)),
        ],
        out_specs=pl.BlockSpec((CONV_TM, D_MODEL), lambda i: (i, 0)),
        out_shape=jax.ShapeDtypeStruct((t, D_MODEL), F32),
        scratch_shapes=[
            pltpu.VMEM((SUBLANES, D_MODEL // LANES, CONV_SH, LANES), F32),
            pltpu.VMEM((CONV_TM, D_MODEL), F32),
        ],
        compiler_params=_params(1),
        name="conformer_conv",
    )(x, x, x, g, w1, b1, wdw, bdw, lng, lnb, w2, b2)


def _trunk(x, attn_w_qkv, attn_b_qkv, attn_tab, attn_w_o, attn_b_o,
           conv_w_pw1, conv_b_pw1, conv_w_dw, conv_b_dw, conv_ln_g, conv_ln_b,
           conv_w_pw2, conv_b_pw2, ffn_w_up, ffn_w_dw, ffn_b_dw, ffn_w_down,
           norm_mix, norm_ffn, norm_final):
    depth = norm_mix.shape[0]
    for i in range(depth):
        j = i // 2
        if i % 2 == 0:
            qkv = _qkv(x, norm_mix[i][None], attn_w_qkv[j], attn_b_qkv[j][None])
            o = _attn(qkv, attn_tab[j])
            x = _proj(o, x, attn_w_o[j], attn_b_o[j][None])
        else:
            x = _cconv(x, norm_mix[i][None], conv_w_pw1[j], conv_b_pw1[j][None],
                       conv_w_dw[j], conv_b_dw[j][None], conv_ln_g[j][None],
                       conv_ln_b[j][None], conv_w_pw2[j], conv_b_pw2[j][None])
        x = _ffn(x, norm_ffn[i][None], ffn_w_up[i], ffn_w_dw[i], ffn_b_dw[i][None],
                 ffn_w_down[i], norm_final[None], final=(i == depth - 1))
    return x


def kernel(x_prompt, x_sample, attn_w_qkv, attn_b_qkv, attn_rpb, attn_w_o, attn_b_o,
           conv_w_pw1, conv_b_pw1, conv_w_dw, conv_b_dw, conv_ln_g, conv_ln_b,
           conv_w_pw2, conv_b_pw2, ffn_w_up, ffn_w_dw, ffn_b_dw, ffn_w_down,
           norm_mix, norm_ffn, norm_final):
    assert x_prompt.shape[1:] == (SEQ, D_MODEL) and x_sample.shape[1:] == (SEQ, D_MODEL)
    tab = jnp.stack([_bias_table(attn_rpb[j]) for j in range(attn_rpb.shape[0])])
    weights = (
        attn_w_qkv.astype(BF16), attn_b_qkv, tab, attn_w_o.astype(BF16), attn_b_o,
        conv_w_pw1.astype(BF16), conv_b_pw1, conv_w_dw, conv_b_dw, conv_ln_g, conv_ln_b,
        conv_w_pw2.astype(BF16), conv_b_pw2, ffn_w_up.astype(BF16), ffn_w_dw, ffn_b_dw,
        ffn_w_down.astype(BF16), norm_mix, norm_ffn, norm_final)
    outs = []
    for x in (x_prompt, x_sample):
        y = _trunk(x.reshape(-1, D_MODEL), *weights)
        outs.append(y.reshape(x.shape))
    return tuple(outs)
```

```python
import functools

import numpy as np
import jax
import jax.numpy as jnp
from jax import lax
from jax.experimental import pallas as pl
from jax.experimental.pallas import tpu as pltpu

D_MODEL = 1024
SEQ = 16384
GRID_W = 64
ROWS = SEQ // GRID_W
N_HEADS = 16
HEAD_DIM = D_MODEL // N_HEADS
WIN_H = 8
WIN_W = 16
CONV_K = 31
FFN_CONV_K = 3
D_FF = 2816
RMS_EPS = 1e-6
LN_EPS = 1e-5

LANES = 128
SUBLANES = 8
N_PAIRS = D_MODEL // LANES
VMEM_LIMIT = 56 * 1024 * 1024

QKV_TM = 512
ATTN_RB = 64
ATTN_HALO = 8
ATTN_MM_UNROLL = 8
ATTN_SM_UNROLL = 2
PROJ_TM = 512
FFN_TM = 512
FFN_HALO = SUBLANES
FFN_FC = 2 * LANES
CONV_TM = 512
CONV_HALO = 16
CONV_CC = 256
CONV_R = 64
CONV_SH = CONV_TM + 2 * CONV_HALO - SUBLANES

BF16 = jnp.bfloat16
F32 = jnp.float32


def _rms(x, g):
    ms = jnp.mean(x * x, axis=-1, keepdims=True)
    return (x * lax.rsqrt(ms + RMS_EPS)) * g


def _params(n_grid):
    return pltpu.CompilerParams(
        dimension_semantics=("arbitrary",) * n_grid, vmem_limit_bytes=VMEM_LIMIT)


def _whole(shape):
    return pl.BlockSpec(shape, lambda *_: (0,) * len(shape), pipeline_mode=pl.Buffered(1))


def _qkv_kernel(x_ref, g_ref, w_ref, b_ref, o_ref):
    hb = _rms(x_ref[...], g_ref[...]).astype(BF16)
    nc = 2 * LANES
    for n in range(3 * D_MODEL // nc):
        acc = jnp.dot(hb, w_ref[:, n * nc:(n + 1) * nc], preferred_element_type=F32)
        acc = acc + b_ref[:, n * nc:(n + 1) * nc]
        if n * nc < D_MODEL:
            acc = acc * (HEAD_DIM ** -0.5)
        o_ref[2 * n] = acc[:, :LANES].astype(BF16)
        o_ref[2 * n + 1] = acc[:, LANES:].astype(BF16)


def _qkv(x, g, w, b):
    t = x.shape[0]
    return pl.pallas_call(
        _qkv_kernel,
        grid=(t // QKV_TM,),
        in_specs=[
            pl.BlockSpec((QKV_TM, D_MODEL), lambda i: (i, 0)),
            _whole((1, D_MODEL)),
            _whole((D_MODEL, 3 * D_MODEL)),
            _whole((1, 3 * D_MODEL)),
        ],
        out_specs=pl.BlockSpec((3 * N_PAIRS, QKV_TM, LANES), lambda i: (0, i, 0)),
        out_shape=jax.ShapeDtypeStruct((3 * N_PAIRS, t, LANES), BF16),
        compiler_params=_params(1),
        name="qkv",
    )(x, g, w, b)


def _attn_kernel(q_ref, kp_ref, km_ref, kn_ref, vp_ref, vm_ref, vn_ref, tab_ref, o_ref,
                 kwin, vwin, vt, s_buf, p_buf, l_buf):
    halo = ATTN_HALO * GRID_W
    main = ATTN_RB * GRID_W
    kwin[0:halo] = kp_ref[0]
    kwin[halo:halo + main] = km_ref[0]
    kwin[halo + main:] = kn_ref[0]
    vwin[0:halo] = vp_ref[0]
    vwin[halo:halo + main] = vm_ref[0]
    vwin[halo + main:] = vn_ref[0]

    for par in range(2):
        for c in range((main + 2 * halo) // LANES - par):
            blk = vwin[par * GRID_W + c * LANES:par * GRID_W + (c + 1) * LANES, :]
            vt[par, :, c * LANES:(c + 1) * LANES] = blk.astype(F32).T.astype(BF16)

    r0 = pl.program_id(2) * ATTN_RB
    low = lax.broadcasted_iota(jnp.int32, (GRID_W, LANES), 1) < HEAD_DIM
    nkeys = WIN_H * GRID_W

    def first_key_row(rr):
        return jnp.clip(r0 + rr - WIN_H // 2, 0, ROWS - WIN_H)

    def scores(rr, carry):
        rs = first_key_row(rr)
        start = pl.multiple_of((rs - r0 + ATTN_HALO) * GRID_W, GRID_W)
        cls = rs - (r0 + rr) + WIN_H - 1
        q = q_ref[0, pl.ds(pl.multiple_of(rr * GRID_W, GRID_W), GRID_W), :]
        zero = jnp.zeros_like(q)
        qs = jnp.concatenate([jnp.where(low, q, zero), jnp.where(low, zero, q)], axis=0)
        kw = kwin[pl.ds(start, nkeys), :]
        st = lax.dot_general(kw, qs, (((1,), (1,)), ((), ())), preferred_element_type=F32)
        bias = tab_ref[0, pl.ds(pl.multiple_of(cls * GRID_W, GRID_W), nkeys), :]
        s_buf[pl.ds(pl.multiple_of(rr * nkeys, nkeys), nkeys), :] = st + bias
        return carry

    lax.fori_loop(0, ATTN_RB, scores, 0, unroll=ATTN_MM_UNROLL)

    nchunk = nkeys // GRID_W

    def softmax(rr, carry):
        base = rr * nkeys

        def chunk(j):
            return pl.ds(pl.multiple_of(base + j * GRID_W, GRID_W), GRID_W)

        acc = s_buf[chunk(0), :]
        for j in range(1, nchunk):
            acc = jnp.maximum(acc, s_buf[chunk(j), :])
        m = jnp.max(acc, axis=0, keepdims=True)
        tot = jnp.zeros((GRID_W, LANES), F32)
        for j in range(nchunk):
            p = jnp.exp(s_buf[chunk(j), :] - m)
            tot = tot + p
            p_buf[chunk(j), :] = p.astype(BF16)
        l = jnp.sum(tot, axis=0, keepdims=True)
        l_buf[pl.ds(pl.multiple_of(rr * SUBLANES, SUBLANES), SUBLANES), :] = (
            jnp.broadcast_to(l, (SUBLANES, LANES)))
        return carry

    lax.fori_loop(0, ATTN_RB, softmax, 0, unroll=ATTN_SM_UNROLL)

    def values(rr, carry):
        wrow = first_key_row(rr) - r0 + ATTN_HALO
        lane0 = pl.multiple_of((wrow // 2) * LANES, LANES)
        vw = vt[wrow % 2, :, pl.ds(lane0, nkeys)]
        pw = p_buf[pl.ds(pl.multiple_of(rr * nkeys, nkeys), nkeys), :]
        ot = jnp.dot(vw, pw, preferred_element_type=F32)
        l = l_buf[pl.ds(pl.multiple_of(rr * SUBLANES, SUBLANES), SUBLANES), :]
        o = (ot / l[0:1]).T
        o = jnp.where(low, o[:GRID_W], o[GRID_W:])
        o_ref[0, pl.ds(pl.multiple_of(rr * GRID_W, GRID_W), GRID_W), :] = o.astype(BF16)
        return carry

    lax.fori_loop(0, ATTN_RB, values, 0, unroll=ATTN_MM_UNROLL)


def _attn(qkv, tab):
    t = qkv.shape[1]
    nseq = t // SEQ
    nblk = ROWS // ATTN_RB
    main = ATTN_RB * GRID_W
    halo = ATTN_HALO * GRID_W
    hps = ROWS // ATTN_HALO
    step = ATTN_RB // ATTN_HALO

    def mainspec(base):
        return pl.BlockSpec((1, main, LANES), lambda p, b, i: (base + p, b * nblk + i, 0))

    def prevspec(base):
        return pl.BlockSpec(
            (1, halo, LANES),
            lambda p, b, i: (base + p, b * hps + jnp.maximum(i * step - 1, 0), 0))

    def nextspec(base):
        return pl.BlockSpec(
            (1, halo, LANES),
            lambda p, b, i: (base + p, b * hps + jnp.minimum((i + 1) * step, hps - 1), 0))

    return pl.pallas_call(
        _attn_kernel,
        grid=(N_PAIRS, nseq, nblk),
        in_specs=[
            mainspec(0),
            prevspec(N_PAIRS), mainspec(N_PAIRS), nextspec(N_PAIRS),
            prevspec(2 * N_PAIRS), mainspec(2 * N_PAIRS), nextspec(2 * N_PAIRS),
            pl.BlockSpec((1, (2 * WIN_H - 1) * GRID_W, 2 * GRID_W), lambda p, b, i: (p, 0, 0)),
        ],
        out_specs=pl.BlockSpec((1, main, LANES), lambda p, b, i: (p, b * nblk + i, 0)),
        out_shape=jax.ShapeDtypeStruct((N_PAIRS, t, LANES), BF16),
        scratch_shapes=[
            pltpu.VMEM((main + 2 * halo, LANES), BF16),
            pltpu.VMEM((main + 2 * halo, LANES), BF16),
            pltpu.VMEM((2, LANES, main + 2 * halo), BF16),
            pltpu.VMEM((ATTN_RB * WIN_H * GRID_W, LANES), F32),
            pltpu.VMEM((ATTN_RB * WIN_H * GRID_W, LANES), BF16),
            pltpu.VMEM((ATTN_RB * SUBLANES, LANES), F32),
        ],
        compiler_params=_params(3),
        name="natten",
    )(qkv, qkv, qkv, qkv, qkv, qkv, qkv, tab)


def _bias_table(rpb):
    qc = np.arange(GRID_W)[:, None]
    kc = np.arange(GRID_W)[None, :]
    ws = np.clip(qc - WIN_W // 2, 0, GRID_W - WIN_W)
    ok = (kc >= ws) & (kc < ws + WIN_W)
    padw = GRID_W - WIN_W
    period = 2 * GRID_W
    padded = jnp.pad(rpb, ((0, 0), (0, 0), (padw, period - padw - rpb.shape[-1])))
    flat = jnp.tile(padded, (1, 1, GRID_W))[..., :GRID_W * (period - 1)]
    t = flat.reshape(rpb.shape[:2] + (GRID_W, period - 1))[..., GRID_W - 1:]
    t = jnp.where(jnp.asarray(ok), t, -jnp.inf)
    ndr = 2 * WIN_H - 1
    tab = t.reshape(N_PAIRS, 2, ndr, GRID_W, GRID_W)
    tab = tab.transpose(0, 2, 4, 1, 3)
    return tab.reshape(N_PAIRS, ndr * GRID_W, 2 * GRID_W)


def _proj_kernel(o_ref, x_ref, w_ref, b_ref, y_ref):
    o = jnp.concatenate([o_ref[p] for p in range(N_PAIRS)], axis=1)
    y = jnp.dot(o, w_ref[...], preferred_element_type=F32)
    y_ref[...] = x_ref[...] + (y + b_ref[...])


def _proj(o, x, w, b):
    t = x.shape[0]
    return pl.pallas_call(
        _proj_kernel,
        grid=(t // PROJ_TM,),
        in_specs=[
            pl.BlockSpec((N_PAIRS, PROJ_TM, LANES), lambda i: (0, i, 0)),
            pl.BlockSpec((PROJ_TM, D_MODEL), lambda i: (i, 0)),
            _whole((D_MODEL, D_MODEL)),
            _whole((1, D_MODEL)),
        ],
        out_specs=pl.BlockSpec((PROJ_TM, D_MODEL), lambda i: (i, 0)),
        out_shape=jax.ShapeDtypeStruct((t, D_MODEL), F32),
        compiler_params=_params(1),
        name="attn_out_proj",
    )(o, x, w, b)


def _halo_specs(tm, halo):
    per = tm // halo

    def prev(i):
        return (jnp.maximum(i * per - 1, 0), 0)

    def nxt(i, nb):
        return (jnp.minimum((i + 1) * per, nb - 1), 0)

    return prev, nxt


def _seq_edges(tm):
    tiles = SEQ // tm
    j = pl.program_id(0) % tiles
    return j == 0, j == tiles - 1


def _ffn_kernel(xp_ref, xm_ref, xn_ref, g_ref, wup_ref, wdw_ref, bdw_ref, wdn_ref, gf_ref,
                y_ref, hb, act, *, final):
    first, last = _seq_edges(FFN_TM)
    g = g_ref[...]
    hp = jnp.where(first, 0.0, _rms(xp_ref[...], g))
    hn = jnp.where(last, 0.0, _rms(xn_ref[...], g))
    hb[...] = jnp.concatenate([hp, _rms(xm_ref[...], g), hn], axis=0).astype(BF16)

    def conv(cols):
        h = jnp.dot(hb[...], wup_ref[:, cols], preferred_element_type=F32)
        w = wdw_ref[:, cols]
        out = bdw_ref[:, cols]
        for k in range(FFN_CONV_K):
            lo = FFN_HALO + k - FFN_CONV_K // 2
            out = out + h[lo:lo + FFN_TM] * w[k:k + 1]
        return out

    def chunk(c, carry):
        gcols = pl.ds(pl.multiple_of(c * FFN_FC, FFN_FC), FFN_FC)
        ucols = pl.ds(pl.multiple_of(D_FF + c * FFN_FC, LANES), FFN_FC)
        gate = conv(gcols)
        up = conv(ucols)
        act[:, gcols] = (gate * jax.nn.sigmoid(gate) * up).astype(BF16)
        return carry

    lax.fori_loop(0, D_FF // FFN_FC, chunk, 0)
    y = xm_ref[...] + jnp.dot(act[...], wdn_ref[...], preferred_element_type=F32)
    if final:
        y = _rms(y, gf_ref[...])
    y_ref[...] = y


def _ffn(x, g, wup, wdw, bdw, wdn, gfinal, final):
    t = x.shape[0]
    prev, nxt = _halo_specs(FFN_TM, FFN_HALO)
    nb = t // FFN_HALO
    return pl.pallas_call(
        functools.partial(_ffn_kernel, final=final),
        grid=(t // FFN_TM,),
        in_specs=[
            pl.BlockSpec((FFN_HALO, D_MODEL), prev),
            pl.BlockSpec((FFN_TM, D_MODEL), lambda i: (i, 0)),
            pl.BlockSpec((FFN_HALO, D_MODEL), lambda i: nxt(i, nb)),
            _whole((1, D_MODEL)),
            _whole((D_MODEL, 2 * D_FF)),
            _whole((FFN_CONV_K, 2 * D_FF)),
            _whole((1, 2 * D_FF)),
            _whole((D_FF, D_MODEL)),
            _whole((1, D_MODEL)),
        ],
        out_specs=pl.BlockSpec((FFN_TM, D_MODEL), lambda i: (i, 0)),
        out_shape=jax.ShapeDtypeStruct((t, D_MODEL), F32),
        scratch_shapes=[
            pltpu.VMEM((FFN_TM + 2 * FFN_HALO, D_MODEL), BF16),
            pltpu.VMEM((FFN_TM, D_FF), BF16),
        ],
        compiler_params=_params(1),
        name="conv_glu_ffn",
    )(x, x, x, g, wup, wdw, bdw, wdn, gfinal)


def _cconv_kernel(xp_ref, xm_ref, xn_ref, g_ref, w1_ref, b1_ref, wdw_ref, bdw_ref,
                  lng_ref, lnb_ref, w2_ref, b2_ref, y_ref, glu, dw):
    first, last = _seq_edges(CONV_TM)
    g = g_ref[...]
    hb = jnp.concatenate(
        [_rms(xp_ref[...], g), _rms(xm_ref[...], g), _rms(xn_ref[...], g)], axis=0
    ).astype(BF16)
    ext = CONV_TM + 2 * CONV_HALO
    row = lax.broadcasted_iota(jnp.int32, (ext, 1), 0)
    pad = (first & (row < CONV_HALO)) | (last & (row >= CONV_HALO + CONV_TM))

    for c in range(D_MODEL // CONV_CC):
        ca, cg = c * CONV_CC, D_MODEL + c * CONV_CC
        a = jnp.dot(hb, w1_ref[:, ca:ca + CONV_CC], preferred_element_type=F32)
        a = a + b1_ref[:, ca:ca + CONV_CC]
        gt = jnp.dot(hb, w1_ref[:, cg:cg + CONV_CC], preferred_element_type=F32)
        gt = gt + b1_ref[:, cg:cg + CONV_CC]
        gl = jnp.where(pad, 0.0, a * jax.nn.sigmoid(gt))
        for b in range(SUBLANES):
            for j in range(CONV_CC // LANES):
                glu[b, ca // LANES + j] = gl[b:b + CONV_SH, j * LANES:(j + 1) * LANES]

    def dw_step(n, carry):
        t0 = (n // (D_MODEL // LANES)) * CONV_R
        lt = n % (D_MODEL // LANES)
        l0 = pl.multiple_of(lt * LANES, LANES)
        w = wdw_ref[:, pl.ds(l0, LANES)]
        acc = jnp.zeros((CONV_R, LANES), F32) + bdw_ref[:, pl.ds(l0, LANES)]
        for k in range(CONV_K):
            a, b = divmod(CONV_HALO + k - CONV_K // 2, SUBLANES)
            rows = pl.ds(pl.multiple_of(t0 + a * SUBLANES, SUBLANES), CONV_R)
            acc = acc + glu[b, lt, rows, :] * w[k:k + 1]
        dw[pl.ds(pl.multiple_of(t0, CONV_R), CONV_R), pl.ds(l0, LANES)] = acc
        return carry

    lax.fori_loop(0, (CONV_TM // CONV_R) * (D_MODEL // LANES), dw_step, 0)

    h = dw[...]
    mu = jnp.mean(h, axis=-1, keepdims=True)
    hc = h - mu
    var = jnp.mean(hc * hc, axis=-1, keepdims=True)
    hn = (hc * lax.rsqrt(var + LN_EPS)) * lng_ref[...] + lnb_ref[...]
    hs = (hn * jax.nn.sigmoid(hn)).astype(BF16)
    y = jnp.dot(hs, w2_ref[...], preferred_element_type=F32) + b2_ref[...]
    y_ref[...] = xm_ref[...] + y


def _cconv(x, g, w1, b1, wdw, bdw, lng, lnb, w2, b2):
    t = x.shape[0]
    prev, nxt = _halo_specs(CONV_TM, CONV_HALO)
    nb = t // CONV_HALO
    return pl.pallas_call(
        _cconv_kernel,
        grid=(t // CONV_TM,),
        in_specs=[
            pl.BlockSpec((CONV_HALO, D_MODEL), prev),
            pl.BlockSpec((CONV_TM, D_MODEL), lambda i: (i, 0)),
            pl.BlockSpec((CONV_HALO, D_MODEL), lambda i: nxt(i, nb)),
            _whole((1, D_MODEL)),
            _whole((D_MODEL, 2 * D_MODEL)),
            _whole((1, 2 * D_MODEL)),
            _whole((CONV_K, D_MODEL)),
            _whole((1, D_MODEL)),
            _whole((1, D_MODEL)),
            _whole((1, D_MODEL)),
            _whole((D_MODEL, D_MODEL)),
            _whole((1, D_MODEL)),
        ],
        out_specs=pl.BlockSpec((CONV_TM, D_MODEL), lambda i: (i, 0)),
        out_shape=jax.ShapeDtypeStruct((t, D_MODEL), F32),
        scratch_shapes=[
            pltpu.VMEM((SUBLANES, D_MODEL // LANES, CONV_SH, LANES), F32),
            pltpu.VMEM((CONV_TM, D_MODEL), F32),
        ],
        compiler_params=_params(1),
        name="conformer_conv",
    )(x, x, x, g, w1, b1, wdw, bdw, lng, lnb, w2, b2)


def _trunk(x, attn_w_qkv, attn_b_qkv, attn_tab, attn_w_o, attn_b_o,
           conv_w_pw1, conv_b_pw1, conv_w_dw, conv_b_dw, conv_ln_g, conv_ln_b,
           conv_w_pw2, conv_b_pw2, ffn_w_up, ffn_w_dw, ffn_b_dw, ffn_w_down,
           norm_mix, norm_ffn, norm_final):
    depth = norm_mix.shape[0]
    for i in range(depth):
        j = i // 2
        if i % 2 == 0:
            qkv = _qkv(x, norm_mix[i][None], attn_w_qkv[j], attn_b_qkv[j][None])
            o = _attn(qkv, attn_tab[j])
            x = _proj(o, x, attn_w_o[j], attn_b_o[j][None])
        else:
            x = _cconv(x, norm_mix[i][None], conv_w_pw1[j], conv_b_pw1[j][None],
                       conv_w_dw[j], conv_b_dw[j][None], conv_ln_g[j][None],
                       conv_ln_b[j][None], conv_w_pw2[j], conv_b_pw2[j][None])
        x = _ffn(x, norm_ffn[i][None], ffn_w_up[i], ffn_w_dw[i], ffn_b_dw[i][None],
                 ffn_w_down[i], norm_final[None], final=(i == depth - 1))
    return x


def kernel(x_prompt, x_sample, attn_w_qkv, attn_b_qkv, attn_rpb, attn_w_o, attn_b_o,
           conv_w_pw1, conv_b_pw1, conv_w_dw, conv_b_dw, conv_ln_g, conv_ln_b,
           conv_w_pw2, conv_b_pw2, ffn_w_up, ffn_w_dw, ffn_b_dw, ffn_w_down,
           norm_mix, norm_ffn, norm_final):
    assert x_prompt.shape[1:] == (SEQ, D_MODEL) and x_sample.shape[1:] == (SEQ, D_MODEL)
    tab = jnp.stack([_bias_table(attn_rpb[j]) for j in range(attn_rpb.shape[0])])
    weights = (
        attn_w_qkv.astype(BF16), attn_b_qkv, tab, attn_w_o.astype(BF16), attn_b_o,
        conv_w_pw1.astype(BF16), conv_b_pw1, conv_w_dw, conv_b_dw, conv_ln_g, conv_ln_b,
        conv_w_pw2.astype(BF16), conv_b_pw2, ffn_w_up.astype(BF16), ffn_w_dw, ffn_b_dw,
        ffn_w_down.astype(BF16), norm_mix, norm_ffn, norm_final)
    outs = []
    for x in (x_prompt, x_sample):
        y = _trunk(x.reshape(-1, D_MODEL), *weights)
        outs.append(y.reshape(x.shape))
    return tuple(outs)
```

```python
import functools

import numpy as np
import jax
import jax.numpy as jnp
from jax import lax
from jax.experimental import pallas as pl
from jax.experimental.pallas import tpu as pltpu

D_MODEL = 1024
SEQ = 16384
GRID_W = 64
ROWS = SEQ // GRID_W
N_HEADS = 16
HEAD_DIM = D_MODEL // N_HEADS
WIN_H = 8
WIN_W = 16
CONV_K = 31
FFN_CONV_K = 3
D_FF = 2816
RMS_EPS = 1e-6
LN_EPS = 1e-5

LANES = 128
SUBLANES = 8
N_PAIRS = D_MODEL // LANES
VMEM_LIMIT = 56 * 1024 * 1024

QKV_TM = 512
ATTN_RB = 64
ATTN_HALO = 8
ATTN_MM_UNROLL = 64
ATTN_SM_UNROLL = 8
PROJ_TM = 512
FFN_TM = 1024
FFN_HALO = SUBLANES
FFN_FC = 2 * LANES
CONV_TM = 512
CONV_HALO = 16
CONV_CC = 256
CONV_R = 128
CONV_SH = CONV_TM + 2 * CONV_HALO - SUBLANES

BF16 = jnp.bfloat16
F32 = jnp.float32


def _rms(x, g):
    ms = jnp.mean(x * x, axis=-1, keepdims=True)
    return (x * lax.rsqrt(ms + RMS_EPS)) * g


def _params(n_grid):
    return pltpu.CompilerParams(
        dimension_semantics=("arbitrary",) * n_grid, vmem_limit_bytes=VMEM_LIMIT)


def _whole(shape):
    return pl.BlockSpec(shape, lambda *_: (0,) * len(shape), pipeline_mode=pl.Buffered(1))


def _qkv_kernel(x_ref, g_ref, w_ref, b_ref, o_ref):
    hb = _rms(x_ref[...], g_ref[...]).astype(BF16)
    nc = 2 * LANES
    for n in range(3 * D_MODEL // nc):
        acc = jnp.dot(hb, w_ref[:, n * nc:(n + 1) * nc], preferred_element_type=F32)
        acc = acc + b_ref[:, n * nc:(n + 1) * nc]
        if n * nc < D_MODEL:
            acc = acc * (HEAD_DIM ** -0.5)
        o_ref[2 * n] = acc[:, :LANES].astype(BF16)
        o_ref[2 * n + 1] = acc[:, LANES:].astype(BF16)


def _qkv(x, g, w, b):
    t = x.shape[0]
    return pl.pallas_call(
        _qkv_kernel,
        grid=(t // QKV_TM,),
        in_specs=[
            pl.BlockSpec((QKV_TM, D_MODEL), lambda i: (i, 0)),
            _whole((1, D_MODEL)),
            _whole((D_MODEL, 3 * D_MODEL)),
            _whole((1, 3 * D_MODEL)),
        ],
        out_specs=pl.BlockSpec((3 * N_PAIRS, QKV_TM, LANES), lambda i: (0, i, 0)),
        out_shape=jax.ShapeDtypeStruct((3 * N_PAIRS, t, LANES), BF16),
        compiler_params=_params(1),
        name="qkv",
    )(x, g, w, b)


def _attn_kernel(q_ref, kp_ref, km_ref, kn_ref, vp_ref, vm_ref, vn_ref, tab_ref, o_ref,
                 kwin, vwin, vt, s_buf, p_buf, l_buf):
    halo = ATTN_HALO * GRID_W
    main = ATTN_RB * GRID_W
    kwin[0:halo] = kp_ref[0]
    kwin[halo:halo + main] = km_ref[0]
    kwin[halo + main:] = kn_ref[0]
    vwin[0:halo] = vp_ref[0]
    vwin[halo:halo + main] = vm_ref[0]
    vwin[halo + main:] = vn_ref[0]

    for par in range(2):
        for c in range((main + 2 * halo) // LANES - par):
            blk = vwin[par * GRID_W + c * LANES:par * GRID_W + (c + 1) * LANES, :]
            vt[par, :, c * LANES:(c + 1) * LANES] = blk.astype(F32).T.astype(BF16)

    r0 = pl.program_id(2) * ATTN_RB
    low = lax.broadcasted_iota(jnp.int32, (GRID_W, LANES), 1) < HEAD_DIM
    nkeys = WIN_H * GRID_W

    def first_key_row(rr):
        return jnp.clip(r0 + rr - WIN_H // 2, 0, ROWS - WIN_H)

    def scores(rr, carry):
        rs = first_key_row(rr)
        start = pl.multiple_of((rs - r0 + ATTN_HALO) * GRID_W, GRID_W)
        cls = rs - (r0 + rr) + WIN_H - 1
        q = q_ref[0, pl.ds(pl.multiple_of(rr * GRID_W, GRID_W), GRID_W), :]
        zero = jnp.zeros_like(q)
        qs = jnp.concatenate([jnp.where(low, q, zero), jnp.where(low, zero, q)], axis=0)
        kw = kwin[pl.ds(start, nkeys), :]
        st = lax.dot_general(kw, qs, (((1,), (1,)), ((), ())), preferred_element_type=F32)
        bias = tab_ref[0, pl.ds(pl.multiple_of(cls * GRID_W, GRID_W), nkeys), :]
        s_buf[pl.ds(pl.multiple_of(rr * nkeys, nkeys), nkeys), :] = st + bias
        return carry

    lax.fori_loop(0, ATTN_RB, scores, 0, unroll=ATTN_MM_UNROLL)

    nchunk = nkeys // GRID_W

    def softmax(rr, carry):
        base = rr * nkeys

        def chunk(j):
            return pl.ds(pl.multiple_of(base + j * GRID_W, GRID_W), GRID_W)

        acc = s_buf[chunk(0), :]
        for j in range(1, nchunk):
            acc = jnp.maximum(acc, s_buf[chunk(j), :])
        m = jnp.max(acc, axis=0, keepdims=True)
        tot = jnp.zeros((GRID_W, LANES), F32)
        for j in range(nchunk):
            p = jnp.exp(s_buf[chunk(j), :] - m)
            tot = tot + p
            p_buf[chunk(j), :] = p.astype(BF16)
        l = jnp.sum(tot, axis=0, keepdims=True)
        l_buf[pl.ds(pl.multiple_of(rr * SUBLANES, SUBLANES), SUBLANES), :] = (
            jnp.broadcast_to(l, (SUBLANES, LANES)))
        return carry

    lax.fori_loop(0, ATTN_RB, softmax, 0, unroll=ATTN_SM_UNROLL)

    def values(rr, carry):
        wrow = first_key_row(rr) - r0 + ATTN_HALO
        lane0 = pl.multiple_of((wrow // 2) * LANES, LANES)
        vw = vt[wrow % 2, :, pl.ds(lane0, nkeys)]
        pw = p_buf[pl.ds(pl.multiple_of(rr * nkeys, nkeys), nkeys), :]
        ot = jnp.dot(vw, pw, preferred_element_type=F32)
        l = l_buf[pl.ds(pl.multiple_of(rr * SUBLANES, SUBLANES), SUBLANES), :]
        o = (ot / l[0:1]).T
        o = jnp.where(low, o[:GRID_W], o[GRID_W:])
        o_ref[0, pl.ds(pl.multiple_of(rr * GRID_W, GRID_W), GRID_W), :] = o.astype(BF16)
        return carry

    lax.fori_loop(0, ATTN_RB, values, 0, unroll=ATTN_MM_UNROLL)


def _attn(qkv, tab):
    t = qkv.shape[1]
    nseq = t // SEQ
    nblk = ROWS // ATTN_RB
    main = ATTN_RB * GRID_W
    halo = ATTN_HALO * GRID_W
    hps = ROWS // ATTN_HALO
    step = ATTN_RB // ATTN_HALO

    def mainspec(base):
        return pl.BlockSpec((1, main, LANES), lambda p, b, i: (base + p, b * nblk + i, 0))

    def prevspec(base):
        return pl.BlockSpec(
            (1, halo, LANES),
            lambda p, b, i: (base + p, b * hps + jnp.maximum(i * step - 1, 0), 0))

    def nextspec(base):
        return pl.BlockSpec(
            (1, halo, LANES),
            lambda p, b, i: (base + p, b * hps + jnp.minimum((i + 1) * step, hps - 1), 0))

    return pl.pallas_call(
        _attn_kernel,
        grid=(N_PAIRS, nseq, nblk),
        in_specs=[
            mainspec(0),
            prevspec(N_PAIRS), mainspec(N_PAIRS), nextspec(N_PAIRS),
            prevspec(2 * N_PAIRS), mainspec(2 * N_PAIRS), nextspec(2 * N_PAIRS),
            pl.BlockSpec((1, (2 * WIN_H - 1) * GRID_W, 2 * GRID_W), lambda p, b, i: (p, 0, 0)),
        ],
        out_specs=pl.BlockSpec((1, main, LANES), lambda p, b, i: (p, b * nblk + i, 0)),
        out_shape=jax.ShapeDtypeStruct((N_PAIRS, t, LANES), BF16),
        scratch_shapes=[
            pltpu.VMEM((main + 2 * halo, LANES), BF16),
            pltpu.VMEM((main + 2 * halo, LANES), BF16),
            pltpu.VMEM((2, LANES, main + 2 * halo), BF16),
            pltpu.VMEM((ATTN_RB * WIN_H * GRID_W, LANES), F32),
            pltpu.VMEM((ATTN_RB * WIN_H * GRID_W, LANES), BF16),
            pltpu.VMEM((ATTN_RB * SUBLANES, LANES), F32),
        ],
        compiler_params=_params(3),
        name="natten",
    )(qkv, qkv, qkv, qkv, qkv, qkv, qkv, tab)


def _bias_table(rpb):
    qc = np.arange(GRID_W)[:, None]
    kc = np.arange(GRID_W)[None, :]
    ws = np.clip(qc - WIN_W // 2, 0, GRID_W - WIN_W)
    ok = (kc >= ws) & (kc < ws + WIN_W)
    padw = GRID_W - WIN_W
    period = 2 * GRID_W
    padded = jnp.pad(rpb, ((0, 0), (0, 0), (padw, period - padw - rpb.shape[-1])))
    flat = jnp.tile(padded, (1, 1, GRID_W))[..., :GRID_W * (period - 1)]
    t = flat.reshape(rpb.shape[:2] + (GRID_W, period - 1))[..., GRID_W - 1:]
    t = jnp.where(jnp.asarray(ok), t, -jnp.inf)
    ndr = 2 * WIN_H - 1
    tab = t.reshape(N_PAIRS, 2, ndr, GRID_W, GRID_W)
    tab = tab.transpose(0, 2, 4, 1, 3)
    return tab.reshape(N_PAIRS, ndr * GRID_W, 2 * GRID_W)


def _proj_kernel(o_ref, x_ref, w_ref, b_ref, y_ref):
    o = jnp.concatenate([o_ref[p] for p in range(N_PAIRS)], axis=1)
    y = jnp.dot(o, w_ref[...], preferred_element_type=F32)
    y_ref[...] = x_ref[...] + (y + b_ref[...])


def _proj(o, x, w, b):
    t = x.shape[0]
    return pl.pallas_call(
        _proj_kernel,
        grid=(t // PROJ_TM,),
        in_specs=[
            pl.BlockSpec((N_PAIRS, PROJ_TM, LANES), lambda i: (0, i, 0)),
            pl.BlockSpec((PROJ_TM, D_MODEL), lambda i: (i, 0)),
            _whole((D_MODEL, D_MODEL)),
            _whole((1, D_MODEL)),
        ],
        out_specs=pl.BlockSpec((PROJ_TM, D_MODEL), lambda i: (i, 0)),
        out_shape=jax.ShapeDtypeStruct((t, D_MODEL), F32),
        compiler_params=_params(1),
        name="attn_out_proj",
    )(o, x, w, b)


def _halo_specs(tm, halo):
    per = tm // halo

    def prev(i):
        return (jnp.maximum(i * per - 1, 0), 0)

    def nxt(i, nb):
        return (jnp.minimum((i + 1) * per, nb - 1), 0)

    return prev, nxt


def _seq_edges(tm):
    tiles = SEQ // tm
    j = pl.program_id(0) % tiles
    return j == 0, j == tiles - 1


def _ffn_kernel(xp_ref, xm_ref, xn_ref, g_ref, wup_ref, wdw_ref, bdw_ref, wdn_ref, gf_ref,
                y_ref, hb, act, *, final):
    first, last = _seq_edges(FFN_TM)
    g = g_ref[...]
    hp = jnp.where(first, 0.0, _rms(xp_ref[...], g))
    hn = jnp.where(last, 0.0, _rms(xn_ref[...], g))
    hb[...] = jnp.concatenate([hp, _rms(xm_ref[...], g), hn], axis=0).astype(BF16)

    def conv(cols):
        h = jnp.dot(hb[...], wup_ref[:, cols], preferred_element_type=F32)
        w = wdw_ref[:, cols]
        out = bdw_ref[:, cols]
        for k in range(FFN_CONV_K):
            lo = FFN_HALO + k - FFN_CONV_K // 2
            out = out + h[lo:lo + FFN_TM] * w[k:k + 1]
        return out

    def chunk(c, carry):
        gcols = pl.ds(pl.multiple_of(c * FFN_FC, FFN_FC), FFN_FC)
        ucols = pl.ds(pl.multiple_of(D_FF + c * FFN_FC, LANES), FFN_FC)
        gate = conv(gcols)
        up = conv(ucols)
        act[:, gcols] = (gate * jax.nn.sigmoid(gate) * up).astype(BF16)
        return carry

    lax.fori_loop(0, D_FF // FFN_FC, chunk, 0)
    y = xm_ref[...] + jnp.dot(act[...], wdn_ref[...], preferred_element_type=F32)
    if final:
        y = _rms(y, gf_ref[...])
    y_ref[...] = y


def _ffn(x, g, wup, wdw, bdw, wdn, gfinal, final):
    t = x.shape[0]
    prev, nxt = _halo_specs(FFN_TM, FFN_HALO)
    nb = t // FFN_HALO
    return pl.pallas_call(
        functools.partial(_ffn_kernel, final=final),
        grid=(t // FFN_TM,),
        in_specs=[
            pl.BlockSpec((FFN_HALO, D_MODEL), prev),
            pl.BlockSpec((FFN_TM, D_MODEL), lambda i: (i, 0)),
            pl.BlockSpec((FFN_HALO, D_MODEL), lambda i: nxt(i, nb)),
            _whole((1, D_MODEL)),
            _whole((D_MODEL, 2 * D_FF)),
            _whole((FFN_CONV_K, 2 * D_FF)),
            _whole((1, 2 * D_FF)),
            _whole((D_FF, D_MODEL)),
            _whole((1, D_MODEL)),
        ],
        out_specs=pl.BlockSpec((FFN_TM, D_MODEL), lambda i: (i, 0)),
        out_shape=jax.ShapeDtypeStruct((t, D_MODEL), F32),
        scratch_shapes=[
            pltpu.VMEM((FFN_TM + 2 * FFN_HALO, D_MODEL), BF16),
            pltpu.VMEM((FFN_TM, D_FF), BF16),
        ],
        compiler_params=_params(1),
        name="conv_glu_ffn",
    )(x, x, x, g, wup, wdw, bdw, wdn, gfinal)


def _cconv_kernel(xp_ref, xm_ref, xn_ref, g_ref, w1_ref, b1_ref, wdw_ref, bdw_ref,
                  lng_ref, lnb_ref, w2_ref, b2_ref, y_ref, glu, dw):
    first, last = _seq_edges(CONV_TM)
    g = g_ref[...]
    hb = jnp.concatenate(
        [_rms(xp_ref[...], g), _rms(xm_ref[...], g), _rms(xn_ref[...], g)], axis=0
    ).astype(BF16)
    ext = CONV_TM + 2 * CONV_HALO
    row = lax.broadcasted_iota(jnp.int32, (ext, 1), 0)
    pad = (first & (row < CONV_HALO)) | (last & (row >= CONV_HALO + CONV_TM))

    for c in range(D_MODEL // CONV_CC):
        ca, cg = c * CONV_CC, D_MODEL + c * CONV_CC
        a = jnp.dot(hb, w1_ref[:, ca:ca + CONV_CC], preferred_element_type=F32)
        a = a + b1_ref[:, ca:ca + CONV_CC]
        gt = jnp.dot(hb, w1_ref[:, cg:cg + CONV_CC], preferred_element_type=F32)
        gt = gt + b1_ref[:, cg:cg + CONV_CC]
        gl = jnp.where(pad, 0.0, a * jax.nn.sigmoid(gt))
        for b in range(SUBLANES):
            for j in range(CONV_CC // LANES):
                glu[b, ca // LANES + j] = gl[b:b + CONV_SH, j * LANES:(j + 1) * LANES]

    def dw_step(n, carry):
        t0 = (n // (D_MODEL // LANES)) * CONV_R
        lt = n % (D_MODEL // LANES)
        l0 = pl.multiple_of(lt * LANES, LANES)
        w = wdw_ref[:, pl.ds(l0, LANES)]
        acc = jnp.zeros((CONV_R, LANES), F32) + bdw_ref[:, pl.ds(l0, LANES)]
        for k in range(CONV_K):
            a, b = divmod(CONV_HALO + k - CONV_K // 2, SUBLANES)
            rows = pl.ds(pl.multiple_of(t0 + a * SUBLANES, SUBLANES), CONV_R)
            acc = acc + glu[b, lt, rows, :] * w[k:k + 1]
        dw[pl.ds(pl.multiple_of(t0, CONV_R), CONV_R), pl.ds(l0, LANES)] = acc
        return carry

    lax.fori_loop(0, (CONV_TM // CONV_R) * (D_MODEL // LANES), dw_step, 0)

    h = dw[...]
    mu = jnp.mean(h, axis=-1, keepdims=True)
    hc = h - mu
    var = jnp.mean(hc * hc, axis=-1, keepdims=True)
    hn = (hc * lax.rsqrt(var + LN_EPS)) * lng_ref[...] + lnb_ref[...]
    hs = (hn * jax.nn.sigmoid(hn)).astype(BF16)
    y = jnp.dot(hs, w2_ref[...], preferred_element_type=F32) + b2_ref[...]
    y_ref[...] = xm_ref[...] + y


def _cconv(x, g, w1, b1, wdw, bdw, lng, lnb, w2, b2):
    t = x.shape[0]
    prev, nxt = _halo_specs(CONV_TM, CONV_HALO)
    nb = t // CONV_HALO
    return pl.pallas_call(
        _cconv_kernel,
        grid=(t // CONV_TM,),
        in_specs=[
            pl.BlockSpec((CONV_HALO, D_MODEL), prev),
            pl.BlockSpec((CONV_TM, D_MODEL), lambda i: (i, 0)),
            pl.BlockSpec((CONV_HALO, D_MODEL), lambda i: nxt(i, nb)),
            _whole((1, D_MODEL)),
            _whole((D_MODEL, 2 * D_MODEL)),
            _whole((1, 2 * D_MODEL)),
            _whole((CONV_K, D_MODEL)),
            _whole((1, D_MODEL)),
            _whole((1, D_MODEL)),
            _whole((1, D_MODEL)),
            _whole((D_MODEL, D_MODEL)),
            _whole((1, D_MODEL)),
        ],
        out_specs=pl.BlockSpec((CONV_TM, D_MODEL), lambda i: (i, 0)),
        out_shape=jax.ShapeDtypeStruct((t, D_MODEL), F32),
        scratch_shapes=[
            pltpu.VMEM((SUBLANES, D_MODEL // LANES, CONV_SH, LANES), F32),
            pltpu.VMEM((CONV_TM, D_MODEL), F32),
        ],
        compiler_params=_params(1),
        name="conformer_conv",
    )(x, x, x, g, w1, b1, wdw, bdw, lng, lnb, w2, b2)


def _trunk(x, attn_w_qkv, attn_b_qkv, attn_tab, attn_w_o, attn_b_o,
           conv_w_pw1, conv_b_pw1, conv_w_dw, conv_b_dw, conv_ln_g, conv_ln_b,
           conv_w_pw2, conv_b_pw2, ffn_w_up, ffn_w_dw, ffn_b_dw, ffn_w_down,
           norm_mix, norm_ffn, norm_final):
    depth = norm_mix.shape[0]
    for i in range(depth):
        j = i // 2
        if i % 2 == 0:
            qkv = _qkv(x, norm_mix[i][None], attn_w_qkv[j], attn_b_qkv[j][None])
            o = _attn(qkv, attn_tab[j])
            x = _proj(o, x, attn_w_o[j], attn_b_o[j][None])
        else:
            x = _cconv(x, norm_mix[i][None], conv_w_pw1[j], conv_b_pw1[j][None],
                       conv_w_dw[j], conv_b_dw[j][None], conv_ln_g[j][None],
                       conv_ln_b[j][None], conv_w_pw2[j], conv_b_pw2[j][None])
        x = _ffn(x, norm_ffn[i][None], ffn_w_up[i], ffn_w_dw[i], ffn_b_dw[i][None],
                 ffn_w_down[i], norm_final[None], final=(i == depth - 1))
    return x


def kernel(x_prompt, x_sample, attn_w_qkv, attn_b_qkv, attn_rpb, attn_w_o, attn_b_o,
           conv_w_pw1, conv_b_pw1, conv_w_dw, conv_b_dw, conv_ln_g, conv_ln_b,
           conv_w_pw2, conv_b_pw2, ffn_w_up, ffn_w_dw, ffn_b_dw, ffn_w_down,
           norm_mix, norm_ffn, norm_final):
    assert x_prompt.shape[1:] == (SEQ, D_MODEL) and x_sample.shape[1:] == (SEQ, D_MODEL)
    tab = jnp.stack([_bias_table(attn_rpb[j]) for j in range(attn_rpb.shape[0])])
    weights = (
        attn_w_qkv.astype(BF16), attn_b_qkv, tab, attn_w_o.astype(BF16), attn_b_o,
        conv_w_pw1.astype(BF16), conv_b_pw1, conv_w_dw, conv_b_dw, conv_ln_g, conv_ln_b,
        conv_w_pw2.astype(BF16), conv_b_pw2, ffn_w_up.astype(BF16), ffn_w_dw, ffn_b_dw,
        ffn_w_down.astype(BF16), norm_mix, norm_ffn, norm_final)
    outs = []
    for x in (x_prompt, x_sample):
        y = _trunk(x.reshape(-1, D_MODEL), *weights)
        outs.append(y.reshape(x.shape))
    return tuple(outs)
```

```python
import functools

import numpy as np
import jax
import jax.numpy as jnp
from jax import lax
from jax.experimental import pallas as pl
from jax.experimental.pallas import tpu as pltpu

D_MODEL = 1024
SEQ = 16384
GRID_W = 64
ROWS = SEQ // GRID_W
N_HEADS = 16
HEAD_DIM = D_MODEL // N_HEADS
WIN_H = 8
WIN_W = 16
CONV_K = 31
FFN_CONV_K = 3
D_FF = 2816
RMS_EPS = 1e-6
LN_EPS = 1e-5

LANES = 128
SUBLANES = 8
N_PAIRS = D_MODEL // LANES
VMEM_LIMIT = 56 * 1024 * 1024

QKV_TM = 512
ATTN_RB = 64
ATTN_HALO = 8
ATTN_MM_UNROLL = 64
ATTN_SM_UNROLL = 64
PROJ_TM = 512
FFN_TM = 1024
FFN_HALO = SUBLANES
FFN_FC = 2 * LANES
CONV_TM = 512
CONV_HALO = 16
CONV_CC = 256
CONV_R = 128
CONV_SH = CONV_TM + 2 * CONV_HALO - SUBLANES

BF16 = jnp.bfloat16
F32 = jnp.float32


def _rms(x, g):
    ms = jnp.mean(x * x, axis=-1, keepdims=True)
    return (x * lax.rsqrt(ms + RMS_EPS)) * g


def _params(n_grid):
    return pltpu.CompilerParams(
        dimension_semantics=("arbitrary",) * n_grid, vmem_limit_bytes=VMEM_LIMIT)


def _whole(shape):
    return pl.BlockSpec(shape, lambda *_: (0,) * len(shape), pipeline_mode=pl.Buffered(1))


def _qkv_kernel(x_ref, g_ref, w_ref, b_ref, o_ref):
    hb = _rms(x_ref[...], g_ref[...]).astype(BF16)
    nc = 2 * LANES
    for n in range(3 * D_MODEL // nc):
        acc = jnp.dot(hb, w_ref[:, n * nc:(n + 1) * nc], preferred_element_type=F32)
        acc = acc + b_ref[:, n * nc:(n + 1) * nc]
        if n * nc < D_MODEL:
            acc = acc * (HEAD_DIM ** -0.5)
        o_ref[2 * n] = acc[:, :LANES].astype(BF16)
        o_ref[2 * n + 1] = acc[:, LANES:].astype(BF16)


def _qkv(x, g, w, b):
    t = x.shape[0]
    return pl.pallas_call(
        _qkv_kernel,
        grid=(t // QKV_TM,),
        in_specs=[
            pl.BlockSpec((QKV_TM, D_MODEL), lambda i: (i, 0)),
            _whole((1, D_MODEL)),
            _whole((D_MODEL, 3 * D_MODEL)),
            _whole((1, 3 * D_MODEL)),
        ],
        out_specs=pl.BlockSpec((3 * N_PAIRS, QKV_TM, LANES), lambda i: (0, i, 0)),
        out_shape=jax.ShapeDtypeStruct((3 * N_PAIRS, t, LANES), BF16),
        compiler_params=_params(1),
        name="qkv",
    )(x, g, w, b)


def _attn_kernel(q_ref, kp_ref, km_ref, kn_ref, vp_ref, vm_ref, vn_ref, tab_ref, o_ref,
                 kwin, vwin, vt, s_buf, p_buf, l_buf):
    halo = ATTN_HALO * GRID_W
    main = ATTN_RB * GRID_W
    kwin[0:halo] = kp_ref[0]
    kwin[halo:halo + main] = km_ref[0]
    kwin[halo + main:] = kn_ref[0]
    vwin[0:halo] = vp_ref[0]
    vwin[halo:halo + main] = vm_ref[0]
    vwin[halo + main:] = vn_ref[0]

    for par in range(2):
        for c in range((main + 2 * halo) // LANES - par):
            blk = vwin[par * GRID_W + c * LANES:par * GRID_W + (c + 1) * LANES, :]
            vt[par, :, c * LANES:(c + 1) * LANES] = blk.astype(F32).T.astype(BF16)

    r0 = pl.program_id(2) * ATTN_RB
    low = lax.broadcasted_iota(jnp.int32, (GRID_W, LANES), 1) < HEAD_DIM
    nkeys = WIN_H * GRID_W

    def first_key_row(rr):
        return jnp.clip(r0 + rr - WIN_H // 2, 0, ROWS - WIN_H)

    def scores(rr, carry):
        rs = first_key_row(rr)
        start = pl.multiple_of((rs - r0 + ATTN_HALO) * GRID_W, GRID_W)
        cls = rs - (r0 + rr) + WIN_H - 1
        q = q_ref[0, pl.ds(pl.multiple_of(rr * GRID_W, GRID_W), GRID_W), :]
        zero = jnp.zeros_like(q)
        qs = jnp.concatenate([jnp.where(low, q, zero), jnp.where(low, zero, q)], axis=0)
        kw = kwin[pl.ds(start, nkeys), :]
        st = lax.dot_general(kw, qs, (((1,), (1,)), ((), ())), preferred_element_type=F32)
        bias = tab_ref[0, pl.ds(pl.multiple_of(cls * GRID_W, GRID_W), nkeys), :]
        s_buf[pl.ds(pl.multiple_of(rr * nkeys, nkeys), nkeys), :] = st + bias
        return carry

    lax.fori_loop(0, ATTN_RB, scores, 0, unroll=ATTN_MM_UNROLL)

    nchunk = nkeys // GRID_W

    def softmax(rr, carry):
        base = rr * nkeys

        def chunk(j):
            return pl.ds(pl.multiple_of(base + j * GRID_W, GRID_W), GRID_W)

        acc = s_buf[chunk(0), :]
        for j in range(1, nchunk):
            acc = jnp.maximum(acc, s_buf[chunk(j), :])
        m = jnp.max(acc, axis=0, keepdims=True)
        tot = jnp.zeros((GRID_W, LANES), F32)
        for j in range(nchunk):
            p = jnp.exp(s_buf[chunk(j), :] - m)
            tot = tot + p
            p_buf[chunk(j), :] = p.astype(BF16)
        l = jnp.sum(tot, axis=0, keepdims=True)
        l_buf[pl.ds(pl.multiple_of(rr * SUBLANES, SUBLANES), SUBLANES), :] = (
            jnp.broadcast_to(l, (SUBLANES, LANES)))
        return carry

    lax.fori_loop(0, ATTN_RB, softmax, 0, unroll=ATTN_SM_UNROLL)

    def values(rr, carry):
        wrow = first_key_row(rr) - r0 + ATTN_HALO
        lane0 = pl.multiple_of((wrow // 2) * LANES, LANES)
        vw = vt[wrow % 2, :, pl.ds(lane0, nkeys)]
        pw = p_buf[pl.ds(pl.multiple_of(rr * nkeys, nkeys), nkeys), :]
        ot = jnp.dot(vw, pw, preferred_element_type=F32)
        l = l_buf[pl.ds(pl.multiple_of(rr * SUBLANES, SUBLANES), SUBLANES), :]
        o = (ot / l[0:1]).T
        o = jnp.where(low, o[:GRID_W], o[GRID_W:])
        o_ref[0, pl.ds(pl.multiple_of(rr * GRID_W, GRID_W), GRID_W), :] = o.astype(BF16)
        return carry

    lax.fori_loop(0, ATTN_RB, values, 0, unroll=ATTN_MM_UNROLL)


def _attn(qkv, tab):
    t = qkv.shape[1]
    nseq = t // SEQ
    nblk = ROWS // ATTN_RB
    main = ATTN_RB * GRID_W
    halo = ATTN_HALO * GRID_W
    hps = ROWS // ATTN_HALO
    step = ATTN_RB // ATTN_HALO

    def mainspec(base):
        return pl.BlockSpec((1, main, LANES), lambda p, b, i: (base + p, b * nblk + i, 0))

    def prevspec(base):
        return pl.BlockSpec(
            (1, halo, LANES),
            lambda p, b, i: (base + p, b * hps + jnp.maximum(i * step - 1, 0), 0))

    def nextspec(base):
        return pl.BlockSpec(
            (1, halo, LANES),
            lambda p, b, i: (base + p, b * hps + jnp.minimum((i + 1) * step, hps - 1), 0))

    return pl.pallas_call(
        _attn_kernel,
        grid=(N_PAIRS, nseq, nblk),
        in_specs=[
            mainspec(0),
            prevspec(N_PAIRS), mainspec(N_PAIRS), nextspec(N_PAIRS),
            prevspec(2 * N_PAIRS), mainspec(2 * N_PAIRS), nextspec(2 * N_PAIRS),
            pl.BlockSpec((1, (2 * WIN_H - 1) * GRID_W, 2 * GRID_W), lambda p, b, i: (p, 0, 0)),
        ],
        out_specs=pl.BlockSpec((1, main, LANES), lambda p, b, i: (p, b * nblk + i, 0)),
        out_shape=jax.ShapeDtypeStruct((N_PAIRS, t, LANES), BF16),
        scratch_shapes=[
            pltpu.VMEM((main + 2 * halo, LANES), BF16),
            pltpu.VMEM((main + 2 * halo, LANES), BF16),
            pltpu.VMEM((2, LANES, main + 2 * halo), BF16),
            pltpu.VMEM((ATTN_RB * WIN_H * GRID_W, LANES), F32),
            pltpu.VMEM((ATTN_RB * WIN_H * GRID_W, LANES), BF16),
            pltpu.VMEM((ATTN_RB * SUBLANES, LANES), F32),
        ],
        compiler_params=_params(3),
        name="natten",
    )(qkv, qkv, qkv, qkv, qkv, qkv, qkv, tab)


def _bias_table(rpb):
    qc = np.arange(GRID_W)[:, None]
    kc = np.arange(GRID_W)[None, :]
    ws = np.clip(qc - WIN_W // 2, 0, GRID_W - WIN_W)
    ok = (kc >= ws) & (kc < ws + WIN_W)
    padw = GRID_W - WIN_W
    period = 2 * GRID_W
    padded = jnp.pad(rpb, ((0, 0), (0, 0), (padw, period - padw - rpb.shape[-1])))
    flat = jnp.tile(padded, (1, 1, GRID_W))[..., :GRID_W * (period - 1)]
    t = flat.reshape(rpb.shape[:2] + (GRID_W, period - 1))[..., GRID_W - 1:]
    t = jnp.where(jnp.asarray(ok), t, -jnp.inf)
    ndr = 2 * WIN_H - 1
    tab = t.reshape(N_PAIRS, 2, ndr, GRID_W, GRID_W)
    tab = tab.transpose(0, 2, 4, 1, 3)
    return tab.reshape(N_PAIRS, ndr * GRID_W, 2 * GRID_W)


def _proj_kernel(o_ref, x_ref, w_ref, b_ref, y_ref):
    o = jnp.concatenate([o_ref[p] for p in range(N_PAIRS)], axis=1)
    y = jnp.dot(o, w_ref[...], preferred_element_type=F32)
    y_ref[...] = x_ref[...] + (y + b_ref[...])


def _proj(o, x, w, b):
    t = x.shape[0]
    return pl.pallas_call(
        _proj_kernel,
        grid=(t // PROJ_TM,),
        in_specs=[
            pl.BlockSpec((N_PAIRS, PROJ_TM, LANES), lambda i: (0, i, 0)),
            pl.BlockSpec((PROJ_TM, D_MODEL), lambda i: (i, 0)),
            _whole((D_MODEL, D_MODEL)),
            _whole((1, D_MODEL)),
        ],
        out_specs=pl.BlockSpec((PROJ_TM, D_MODEL), lambda i: (i, 0)),
        out_shape=jax.ShapeDtypeStruct((t, D_MODEL), F32),
        compiler_params=_params(1),
        name="attn_out_proj",
    )(o, x, w, b)


def _halo_specs(tm, halo):
    per = tm // halo

    def prev(i):
        return (jnp.maximum(i * per - 1, 0), 0)

    def nxt(i, nb):
        return (jnp.minimum((i + 1) * per, nb - 1), 0)

    return prev, nxt


def _seq_edges(tm):
    tiles = SEQ // tm
    j = pl.program_id(0) % tiles
    return j == 0, j == tiles - 1


def _ffn_kernel(xp_ref, xm_ref, xn_ref, g_ref, wup_ref, wdw_ref, bdw_ref, wdn_ref, gf_ref,
                y_ref, hb, act, *, final):
    first, last = _seq_edges(FFN_TM)
    g = g_ref[...]
    hp = jnp.where(first, 0.0, _rms(xp_ref[...], g))
    hn = jnp.where(last, 0.0, _rms(xn_ref[...], g))
    hb[...] = jnp.concatenate([hp, _rms(xm_ref[...], g), hn], axis=0).astype(BF16)

    def conv(cols):
        h = jnp.dot(hb[...], wup_ref[:, cols], preferred_element_type=F32)
        w = wdw_ref[:, cols]
        out = bdw_ref[:, cols]
        for k in range(FFN_CONV_K):
            lo = FFN_HALO + k - FFN_CONV_K // 2
            out = out + h[lo:lo + FFN_TM] * w[k:k + 1]
        return out

    def chunk(c, carry):
        gcols = pl.ds(pl.multiple_of(c * FFN_FC, FFN_FC), FFN_FC)
        ucols = pl.ds(pl.multiple_of(D_FF + c * FFN_FC, LANES), FFN_FC)
        gate = conv(gcols)
        up = conv(ucols)
        act[:, gcols] = (gate * jax.nn.sigmoid(gate) * up).astype(BF16)
        return carry

    lax.fori_loop(0, D_FF // FFN_FC, chunk, 0)
    y = xm_ref[...] + jnp.dot(act[...], wdn_ref[...], preferred_element_type=F32)
    if final:
        y = _rms(y, gf_ref[...])
    y_ref[...] = y


def _ffn(x, g, wup, wdw, bdw, wdn, gfinal, final):
    t = x.shape[0]
    prev, nxt = _halo_specs(FFN_TM, FFN_HALO)
    nb = t // FFN_HALO
    return pl.pallas_call(
        functools.partial(_ffn_kernel, final=final),
        grid=(t // FFN_TM,),
        in_specs=[
            pl.BlockSpec((FFN_HALO, D_MODEL), prev),
            pl.BlockSpec((FFN_TM, D_MODEL), lambda i: (i, 0)),
            pl.BlockSpec((FFN_HALO, D_MODEL), lambda i: nxt(i, nb)),
            _whole((1, D_MODEL)),
            _whole((D_MODEL, 2 * D_FF)),
            _whole((FFN_CONV_K, 2 * D_FF)),
            _whole((1, 2 * D_FF)),
            _whole((D_FF, D_MODEL)),
            _whole((1, D_MODEL)),
        ],
        out_specs=pl.BlockSpec((FFN_TM, D_MODEL), lambda i: (i, 0)),
        out_shape=jax.ShapeDtypeStruct((t, D_MODEL), F32),
        scratch_shapes=[
            pltpu.VMEM((FFN_TM + 2 * FFN_HALO, D_MODEL), BF16),
            pltpu.VMEM((FFN_TM, D_FF), BF16),
        ],
        compiler_params=_params(1),
        name="conv_glu_ffn",
    )(x, x, x, g, wup, wdw, bdw, wdn, gfinal)


def _cconv_kernel(xp_ref, xm_ref, xn_ref, g_ref, w1_ref, b1_ref, wdw_ref, bdw_ref,
                  lng_ref, lnb_ref, w2_ref, b2_ref, y_ref, glu, dw):
    first, last = _seq_edges(CONV_TM)
    g = g_ref[...]
    hb = jnp.concatenate(
        [_rms(xp_ref[...], g), _rms(xm_ref[...], g), _rms(xn_ref[...], g)], axis=0
    ).astype(BF16)
    ext = CONV_TM + 2 * CONV_HALO
    row = lax.broadcasted_iota(jnp.int32, (ext, 1), 0)
    pad = (first & (row < CONV_HALO)) | (last & (row >= CONV_HALO + CONV_TM))

    for c in range(D_MODEL // CONV_CC):
        ca, cg = c * CONV_CC, D_MODEL + c * CONV_CC
        a = jnp.dot(hb, w1_ref[:, ca:ca + CONV_CC], preferred_element_type=F32)
        a = a + b1_ref[:, ca:ca + CONV_CC]
        gt = jnp.dot(hb, w1_ref[:, cg:cg + CONV_CC], preferred_element_type=F32)
        gt = gt + b1_ref[:, cg:cg + CONV_CC]
        gl = jnp.where(pad, 0.0, a * jax.nn.sigmoid(gt))
        for b in range(SUBLANES):
            for j in range(CONV_CC // LANES):
                glu[b, ca // LANES + j] = gl[b:b + CONV_SH, j * LANES:(j + 1) * LANES]

    def dw_step(n, carry):
        t0 = (n // (D_MODEL // LANES)) * CONV_R
        lt = n % (D_MODEL // LANES)
        l0 = pl.multiple_of(lt * LANES, LANES)
        w = wdw_ref[:, pl.ds(l0, LANES)]
        acc = jnp.zeros((CONV_R, LANES), F32) + bdw_ref[:, pl.ds(l0, LANES)]
        for k in range(CONV_K):
            a, b = divmod(CONV_HALO + k - CONV_K // 2, SUBLANES)
            rows = pl.ds(pl.multiple_of(t0 + a * SUBLANES, SUBLANES), CONV_R)
            acc = acc + glu[b, lt, rows, :] * w[k:k + 1]
        dw[pl.ds(pl.multiple_of(t0, CONV_R), CONV_R), pl.ds(l0, LANES)] = acc
        return carry

    lax.fori_loop(0, (CONV_TM // CONV_R) * (D_MODEL // LANES), dw_step, 0)

    h = dw[...]
    mu = jnp.mean(h, axis=-1, keepdims=True)
    hc = h - mu
    var = jnp.mean(hc * hc, axis=-1, keepdims=True)
    hn = (hc * lax.rsqrt(var + LN_EPS)) * lng_ref[...] + lnb_ref[...]
    hs = (hn * jax.nn.sigmoid(hn)).astype(BF16)
    y = jnp.dot(hs, w2_ref[...], preferred_element_type=F32) + b2_ref[...]
    y_ref[...] = xm_ref[...] + y


def _cconv(x, g, w1, b1, wdw, bdw, lng, lnb, w2, b2):
    t = x.shape[0]
    prev, nxt = _halo_specs(CONV_TM, CONV_HALO)
    nb = t // CONV_HALO
    return pl.pallas_call(
        _cconv_kernel,
        grid=(t // CONV_TM,),
        in_specs=[
            pl.BlockSpec((CONV_HALO, D_MODEL), prev),
            pl.BlockSpec((CONV_TM, D_MODEL), lambda i: (i, 0)),
            pl.BlockSpec((CONV_HALO, D_MODEL), lambda i: nxt(i, nb)),
            _whole((1, D_MODEL)),
            _whole((D_MODEL, 2 * D_MODEL)),
            _whole((1, 2 * D_MODEL)),
            _whole((CONV_K, D_MODEL)),
            _whole((1, D_MODEL)),
            _whole((1, D_MODEL)),
            _whole((1, D_MODEL)),
            _whole((D_MODEL, D_MODEL)),
            _whole((1, D_MODEL)),
        ],
        out_specs=pl.BlockSpec((CONV_TM, D_MODEL), lambda i: (i, 0)),
        out_shape=jax.ShapeDtypeStruct((t, D_MODEL), F32),
        scratch_shapes=[
            pltpu.VMEM((SUBLANES, D_MODEL // LANES, CONV_SH, LANES), F32),
            pltpu.VMEM((CONV_TM, D_MODEL), F32),
        ],
        compiler_params=_params(1),
        name="conformer_conv",
    )(x, x, x, g, w1, b1, wdw, bdw, lng, lnb, w2, b2)


def _trunk(x, attn_w_qkv, attn_b_qkv, attn_tab, attn_w_o, attn_b_o,
           conv_w_pw1, conv_b_pw1, conv_w_dw, conv_b_dw, conv_ln_g, conv_ln_b,
           conv_w_pw2, conv_b_pw2, ffn_w_up, ffn_w_dw, ffn_b_dw, ffn_w_down,
           norm_mix, norm_ffn, norm_final):
    depth = norm_mix.shape[0]
    for i in range(depth):
        j = i // 2
        if i % 2 == 0:
            qkv = _qkv(x, norm_mix[i][None], attn_w_qkv[j], attn_b_qkv[j][None])
            o = _attn(qkv, attn_tab[j])
            x = _proj(o, x, attn_w_o[j], attn_b_o[j][None])
        else:
            x = _cconv(x, norm_mix[i][None], conv_w_pw1[j], conv_b_pw1[j][None],
                       conv_w_dw[j], conv_b_dw[j][None], conv_ln_g[j][None],
                       conv_ln_b[j][None], conv_w_pw2[j], conv_b_pw2[j][None])
        x = _ffn(x, norm_ffn[i][None], ffn_w_up[i], ffn_w_dw[i], ffn_b_dw[i][None],
                 ffn_w_down[i], norm_final[None], final=(i == depth - 1))
    return x


def kernel(x_prompt, x_sample, attn_w_qkv, attn_b_qkv, attn_rpb, attn_w_o, attn_b_o,
           conv_w_pw1, conv_b_pw1, conv_w_dw, conv_b_dw, conv_ln_g, conv_ln_b,
           conv_w_pw2, conv_b_pw2, ffn_w_up, ffn_w_dw, ffn_b_dw, ffn_w_down,
           norm_mix, norm_ffn, norm_final):
    assert x_prompt.shape[1:] == (SEQ, D_MODEL) and x_sample.shape[1:] == (SEQ, D_MODEL)
    tab = jnp.stack([_bias_table(attn_rpb[j]) for j in range(attn_rpb.shape[0])])
    weights = (
        attn_w_qkv.astype(BF16), attn_b_qkv, tab, attn_w_o.astype(BF16), attn_b_o,
        conv_w_pw1.astype(BF16), conv_b_pw1, conv_w_dw, conv_b_dw, conv_ln_g, conv_ln_b,
        conv_w_pw2.astype(BF16), conv_b_pw2, ffn_w_up.astype(BF16), ffn_w_dw, ffn_b_dw,
        ffn_w_down.astype(BF16), norm_mix, norm_ffn, norm_final)
    outs = []
    for x in (x_prompt, x_sample):
        y = _trunk(x.reshape(-1, D_MODEL), *weights)
        outs.append(y.reshape(x.shape))
    return tuple(outs)
```
